```python
import math
import jax
import jax.numpy as jnp
from jax import lax
import numpy as np

D_MODEL = 2048
BATCH = 2
SEQ = 8192
DEPTH = 4

N_MIXERS = 2
N_DN_LAYERS = (DEPTH + 1) // 2
N_LRU_LAYERS = DEPTH // 2
RMS_EPS = 1e-6
CONV_WIDTH = 4
CONV_PAD = (2, 1)

DN_KEY_HEADS = 16
DN_VALUE_HEADS = 32
DN_HEAD_DIM = 128
DN_KEY_DIM = DN_KEY_HEADS * DN_HEAD_DIM
DN_VALUE_DIM = DN_VALUE_HEADS * DN_HEAD_DIM
DN_CONV_DIM = 2 * DN_KEY_DIM + DN_VALUE_DIM
DN_IN_DIM = DN_CONV_DIM + DN_VALUE_DIM + 4 * DN_VALUE_HEADS
DN_CHUNK = 64

LRU_WIDTH = D_MODEL
LRU_BLOCKS = 8
LRU_BLOCK_DIM = LRU_WIDTH // LRU_BLOCKS
LRU_C = 8.0

FFN_DIM = 5632

kernel_name = "bidir_hybrid_gdn_rglru_macaron"


def rmsnorm(x, w):
    xf = x.astype(jnp.float32)
    y = xf * lax.rsqrt(jnp.mean(xf * xf, axis=-1, keepdims=True) + RMS_EPS)
    return (y * w.astype(jnp.float32)).astype(x.dtype)


def l2norm(x):
    return x * lax.rsqrt(jnp.sum(x * x, axis=-1, keepdims=True) + RMS_EPS)


def swiglu_ffn(x, w_gate_up, w_down):
    g, u = jnp.split(x @ w_gate_up, 2, axis=-1)
    return (jax.nn.silu(g) * u) @ w_down


def centred_depthwise_conv(x, w):
    return lax.conv_general_dilated(
        x, w[:, None, :].astype(x.dtype), window_strides=(1,), padding=[CONV_PAD],
        dimension_numbers=("NWC", "WIO", "NWC"), feature_group_count=x.shape[-1])


def chunk_gated_delta_rule(q, k, v, g, beta):
    n_b, n_h, seq, d_k = k.shape
    d_v = v.shape[-1]
    n_c = seq // DN_CHUNK
    to_chunks = lambda t: jnp.moveaxis(t.reshape(n_b, n_h, n_c, DN_CHUNK, *t.shape[3:]), 2, 0)
    xs = (to_chunks(q), to_chunks(k), to_chunks(v), to_chunks(g), to_chunks(beta))
    tril = jnp.tril(jnp.ones((DN_CHUNK, DN_CHUNK), dtype=bool))
    strict = jnp.tril(jnp.ones((DN_CHUNK, DN_CHUNK), dtype=bool), -1)
    eye = jnp.eye(DN_CHUNK, dtype=jnp.float32)

    def step(state, inp):
        q_c, k_c, v_c, g_c, b_c = inp
        g_cum = jnp.cumsum(g_c, axis=-1)
        decay = jnp.exp(jnp.where(tril, g_cum[..., :, None] - g_cum[..., None, :], -jnp.inf))
        k_beta = k_c * b_c[..., None]
        a_mat = jnp.where(strict, jnp.einsum('nhid,nhjd->nhij', k_beta, k_c) * decay, 0.0) + eye
        rhs = jnp.concatenate([v_c * b_c[..., None], k_beta * jnp.exp(g_cum)[..., None]], axis=-1)
        sol = lax.linalg.triangular_solve(a_mat, rhs, left_side=True, lower=True, unit_diagonal=True)
        u_c, w_c = sol[..., :d_v], sol[..., d_v:]
        v_new = u_c - jnp.einsum('nhcd,nhde->nhce', w_c, state)
        attn = jnp.where(tril, jnp.einsum('nhid,nhjd->nhij', q_c, k_c) * decay, 0.0)
        o_c = (jnp.einsum('nhcd,nhde->nhce', q_c * jnp.exp(g_cum)[..., None], state)
               + jnp.einsum('nhij,nhje->nhie', attn, v_new))
        g_last = g_cum[..., -1]
        k_dec = k_c * jnp.exp(g_last[..., None] - g_cum)[..., None]
        state = state * jnp.exp(g_last)[..., None, None] + jnp.einsum('nhcd,nhce->nhde', k_dec, v_new)
        return state, o_c

    init = jnp.zeros((n_b, n_h, d_k, d_v), jnp.float32)
    _, o = lax.scan(step, init, xs)
    return jnp.moveaxis(o, 0, 2).reshape(n_b, n_h, seq, d_v)


def gated_deltanet_mixer(x, w_in, conv_w, a_log, dt_bias, norm_w, w_out):
    n_b, seq, _ = x.shape
    f32 = jnp.float32
    proj = x @ w_in
    qkv, z, ba = jnp.split(proj, [DN_CONV_DIM, DN_CONV_DIM + DN_VALUE_DIM], axis=-1)
    qkv = jax.nn.silu(centred_depthwise_conv(qkv, conv_w)).astype(f32)
    q, k, v = jnp.split(qkv, [DN_KEY_DIM, 2 * DN_KEY_DIM], axis=-1)
    rep = DN_VALUE_HEADS // DN_KEY_HEADS
    q = l2norm(q.reshape(n_b, seq, DN_KEY_HEADS, DN_HEAD_DIM)) * (DN_HEAD_DIM ** -0.5)
    k = l2norm(k.reshape(n_b, seq, DN_KEY_HEADS, DN_HEAD_DIM))
    q = jnp.repeat(q, rep, axis=2).transpose(0, 2, 1, 3)
    k = jnp.repeat(k, rep, axis=2).transpose(0, 2, 1, 3)
    v = v.reshape(n_b, seq, DN_VALUE_HEADS, DN_HEAD_DIM).transpose(0, 2, 1, 3)
    ba = ba.astype(f32).reshape(n_b, seq, 2, 2, DN_VALUE_HEADS)
    beta = jax.nn.sigmoid(ba[:, :, :, 0]).transpose(2, 0, 3, 1)
    g = (-jnp.exp(a_log.astype(f32)) * jax.nn.softplus(ba[:, :, :, 1] + dt_bias.astype(f32))).transpose(2, 0, 3, 1)
    flip = lambda t: jnp.flip(t, axis=2)
    o = chunk_gated_delta_rule(
        jnp.concatenate([q, flip(q)], 0), jnp.concatenate([k, flip(k)], 0),
        jnp.concatenate([v, flip(v)], 0), jnp.concatenate([g[0], flip(g[1])], 0),
        jnp.concatenate([beta[0], flip(beta[1])], 0))
    o = (o[:n_b] + flip(o[n_b:])).transpose(0, 2, 1, 3)
    z = z.astype(f32).reshape(n_b, seq, DN_VALUE_HEADS, DN_HEAD_DIM)
    o = rmsnorm(o, norm_w) * jax.nn.silu(z)
    return o.reshape(n_b, seq, DN_VALUE_DIM).astype(x.dtype) @ w_out


def linear_recurrence_op(c1, c2):
    a1, b1 = c1
    a2, b2 = c2
    return a1 * a2, a2 * b1 + b2


def rglru_mixer(x, w_in, conv_w, conv_b, w_gate_a, b_gate_a, w_gate_x, b_gate_x, lam, w_out):
    n_b, seq, _ = x.shape
    f32 = jnp.float32
    xb, gate = jnp.split(x @ w_in, 2, axis=-1)
    xb = (centred_depthwise_conv(xb, conv_w) + conv_b.astype(xb.dtype)).astype(f32)
    xh = xb.reshape(n_b, seq, LRU_BLOCKS, LRU_BLOCK_DIM)
    r = jax.nn.sigmoid(jnp.einsum('bshi,dhij->dbshj', xh, w_gate_a.astype(f32)).reshape(2, n_b, seq, LRU_WIDTH)
                       + b_gate_a.astype(f32)[:, None, None, :])
    i = jax.nn.sigmoid(jnp.einsum('bshi,dhij->dbshj', xh, w_gate_x.astype(f32)).reshape(2, n_b, seq, LRU_WIDTH)
                       + b_gate_x.astype(f32)[:, None, None, :])
    log_a = -LRU_C * r * jax.nn.softplus(-lam.astype(f32))[:, None, None, :]
    a = jnp.exp(log_a)
    b = jnp.sqrt(-jnp.expm1(2.0 * log_a)) * (i * xb[None])
    _, h_fwd = lax.associative_scan(linear_recurrence_op, (a[0], b[0]), axis=1)
    _, h_bwd = lax.associative_scan(linear_recurrence_op, (a[1], b[1]), axis=1, reverse=True)
    y = (h_fwd + h_bwd) * jax.nn.gelu(gate.astype(f32))
    return y.astype(x.dtype) @ w_out


def setup_inputs(seed: int = 0) -> dict:
    key = jax.random.key(seed)
    ks = jax.random.split(key, 24)
    f32 = jnp.float32
    normal = lambda k, shape, scale: jax.random.normal(k, shape, f32) * scale
    gain = lambda k, shape: 1.0 + 0.01 * jax.random.normal(k, shape, f32)
    x = normal(ks[0], (BATCH, SEQ, D_MODEL), 1.0)
    ffn1_norm = gain(ks[1], (DEPTH, D_MODEL))
    ffn1_w_gate_up = normal(ks[2], (DEPTH, D_MODEL, 2 * FFN_DIM), D_MODEL ** -0.5)
    ffn1_w_down = normal(ks[3], (DEPTH, FFN_DIM, D_MODEL), FFN_DIM ** -0.5)
    mix_norm = gain(ks[4], (DEPTH, D_MODEL))
    ffn2_norm = gain(ks[5], (DEPTH, D_MODEL))
    ffn2_w_gate_up = normal(ks[6], (DEPTH, D_MODEL, 2 * FFN_DIM), D_MODEL ** -0.5)
    ffn2_w_down = normal(ks[7], (DEPTH, FFN_DIM, D_MODEL), FFN_DIM ** -0.5)
    dn_w_in = normal(ks[8], (N_DN_LAYERS, D_MODEL, DN_IN_DIM), D_MODEL ** -0.5)
    dn_conv_w = normal(ks[9], (N_DN_LAYERS, CONV_WIDTH, DN_CONV_DIM), CONV_WIDTH ** -0.5)
    dn_a_log = jnp.log(jax.random.uniform(ks[10], (N_DN_LAYERS, 2, DN_VALUE_HEADS), f32, 1.0, 16.0))
    dt = jnp.exp(jax.random.uniform(ks[11], (N_DN_LAYERS, 2, DN_VALUE_HEADS), f32,
                                    math.log(1e-3), math.log(1e-1)))
    dn_dt_bias = dt + jnp.log(-jnp.expm1(-dt))
    dn_out_norm = gain(ks[12], (N_DN_LAYERS, DN_HEAD_DIM))
    dn_w_out = normal(ks[13], (N_DN_LAYERS, DN_VALUE_DIM, D_MODEL), DN_VALUE_DIM ** -0.5)
    lru_w_in = normal(ks[14], (N_LRU_LAYERS, D_MODEL, 2 * LRU_WIDTH), D_MODEL ** -0.5)
    lru_conv_w = normal(ks[15], (N_LRU_LAYERS, CONV_WIDTH, LRU_WIDTH), CONV_WIDTH ** -0.5)
    lru_conv_b = normal(ks[16], (N_LRU_LAYERS, LRU_WIDTH), 0.01)
    gshape = (N_LRU_LAYERS, 2, LRU_BLOCKS, LRU_BLOCK_DIM, LRU_BLOCK_DIM)
    lru_w_gate_a = normal(ks[17], gshape, LRU_BLOCK_DIM ** -0.5)
    lru_b_gate_a = normal(ks[18], (N_LRU_LAYERS, 2, LRU_WIDTH), 0.01)
    lru_w_gate_x = normal(ks[19], gshape, LRU_BLOCK_DIM ** -0.5)
    lru_b_gate_x = normal(ks[20], (N_LRU_LAYERS, 2, LRU_WIDTH), 0.01)
    u = jax.random.uniform(ks[21], (N_LRU_LAYERS, 2, LRU_WIDTH), f32, 0.9 ** 2, 0.999 ** 2)
    sp = -0.5 * jnp.log(u)
    lru_lambda = -(sp + jnp.log(-jnp.expm1(-sp)))
    lru_w_out = normal(ks[22], (N_LRU_LAYERS, LRU_WIDTH, D_MODEL), LRU_WIDTH ** -0.5)
    final_norm = gain(ks[23], (D_MODEL,))
    return {"x": x, "ffn1_norm": ffn1_norm, "ffn1_w_gate_up": ffn1_w_gate_up, "ffn1_w_down": ffn1_w_down,
            "mix_norm": mix_norm, "ffn2_norm": ffn2_norm, "ffn2_w_gate_up": ffn2_w_gate_up,
            "ffn2_w_down": ffn2_w_down, "dn_w_in": dn_w_in, "dn_conv_w": dn_conv_w, "dn_a_log": dn_a_log,
            "dn_dt_bias": dn_dt_bias, "dn_out_norm": dn_out_norm, "dn_w_out": dn_w_out,
            "lru_w_in": lru_w_in, "lru_conv_w": lru_conv_w, "lru_conv_b": lru_conv_b,
            "lru_w_gate_a": lru_w_gate_a, "lru_b_gate_a": lru_b_gate_a, "lru_w_gate_x": lru_w_gate_x,
            "lru_b_gate_x": lru_b_gate_x, "lru_lambda": lru_lambda, "lru_w_out": lru_w_out,
            "final_norm": final_norm}


def reference(x, ffn1_norm, ffn1_w_gate_up, ffn1_w_down, mix_norm, ffn2_norm, ffn2_w_gate_up, ffn2_w_down,
              dn_w_in, dn_conv_w, dn_a_log, dn_dt_bias, dn_out_norm, dn_w_out,
              lru_w_in, lru_conv_w, lru_conv_b, lru_w_gate_a, lru_b_gate_a, lru_w_gate_x, lru_b_gate_x,
              lru_lambda, lru_w_out, final_norm):
    h = x
    for layer in range(DEPTH):
        h = h + 0.5 * swiglu_ffn(rmsnorm(h, ffn1_norm[layer]), ffn1_w_gate_up[layer], ffn1_w_down[layer])
        hn = rmsnorm(h, mix_norm[layer])
        j = layer // N_MIXERS
        if layer % N_MIXERS == 0:
            h = h + gated_deltanet_mixer(hn, dn_w_in[j], dn_conv_w[j], dn_a_log[j], dn_dt_bias[j],
                                         dn_out_norm[j], dn_w_out[j])
        else:
            h = h + rglru_mixer(hn, lru_w_in[j], lru_conv_w[j], lru_conv_b[j], lru_w_gate_a[j],
                                lru_b_gate_a[j], lru_w_gate_x[j], lru_b_gate_x[j], lru_lambda[j],
                                lru_w_out[j])
        h = h + 0.5 * swiglu_ffn(rmsnorm(h, ffn2_norm[layer]), ffn2_w_gate_up[layer], ffn2_w_down[layer])
    return rmsnorm(h, final_norm)
```

```python
import functools
import math

import jax
import jax.numpy as jnp
from jax import lax
from jax.experimental import pallas as pl
from jax.experimental.pallas import tpu as pltpu

F32 = jnp.float32
BF16 = jnp.bfloat16

RMS_EPS = 1e-6
CONV_WIDTH = 4
CONV_LEFT = 2
DN_CHUNK = 64
HEAD_DIM = 128
LRU_C = 8.0

V7X_LANES = 128
V7X_SUBLANES = 8
V7X_VMEM_BYTES = 64 * 1024 * 1024
V7X_VMEM_USABLE = V7X_VMEM_BYTES - 6 * 1024 * 1024

TM_MATMUL = 1024
TN_MATMUL = 1024
TM_FFN = 512
TF_FFN = 512
TS_CONV = 512
CW_CONV = 512
TS_LRU = 512
DN_HEADS_PER_STEP = 4
DN_CHUNKS_PER_STEP = 16
TM_ELEMWISE = 256


def _tile(extent, preferred, quantum):
    if extent <= preferred:
        return extent
    t = (preferred // quantum) * quantum
    while t >= quantum:
        if extent % t == 0:
            return t
        t -= quantum
    return extent


def _vmem_limit(pipelined, scratch=()):
    nbytes = lambda s, d: math.prod(s) * jnp.dtype(d).itemsize
    total = 2 * sum(nbytes(s, d) for s, d in pipelined) + sum(nbytes(s, d) for s, d in scratch)
    return int(min(V7X_VMEM_USABLE, max(total * 5 // 4, 16 * 1024 * 1024)))


def _params(semantics, vmem):
    return pltpu.CompilerParams(dimension_semantics=semantics, vmem_limit_bytes=vmem)


def _bdot(a, b):
    return jnp.dot(a.astype(BF16), b.astype(BF16), preferred_element_type=F32)


def _rmsnorm_rows(x, w):
    return x * lax.rsqrt(jnp.mean(x * x, axis=-1, keepdims=True) + RMS_EPS) * w


def _softplus(x):
    return jnp.maximum(x, 0.0) + jnp.log1p(jnp.exp(-jnp.abs(x)))


def _silu(x):
    return x * jax.nn.sigmoid(x)


def _gelu_tanh(x):
    return 0.5 * x * (1.0 + jnp.tanh(math.sqrt(2.0 / math.pi) * (x + 0.044715 * (x * x * x))))


def _norm_matmul_kernel(x_ref, nw_ref, w_ref, o_ref, xn_ref):
    @pl.when(pl.program_id(1) == 0)
    def _():
        xn_ref[...] = _rmsnorm_rows(x_ref[...], nw_ref[...]).astype(BF16)

    o_ref[...] = jnp.dot(xn_ref[...], w_ref[...], preferred_element_type=F32)


def _norm_matmul(x, norm_w, w):
    t, k = x.shape
    n = w.shape[1]
    tm = _tile(t, TM_MATMUL, V7X_SUBLANES)
    tn = _tile(n, TN_MATMUL, V7X_LANES)
    vmem = _vmem_limit([((tm, k), F32), ((k, tn), BF16), ((tm, tn), F32)], [((tm, k), BF16)])
    return pl.pallas_call(
        _norm_matmul_kernel,
        name="norm_matmul",
        grid=(t // tm, n // tn),
        in_specs=[pl.BlockSpec((tm, k), lambda i, j: (i, 0)),
                  pl.BlockSpec((1, k), lambda i, j: (0, 0)),
                  pl.BlockSpec((k, tn), lambda i, j: (0, j))],
        out_specs=pl.BlockSpec((tm, tn), lambda i, j: (i, j)),
        out_shape=jax.ShapeDtypeStruct((t, n), F32),
        scratch_shapes=[pltpu.VMEM((tm, k), BF16)],
        compiler_params=_params(("parallel", "arbitrary"), vmem),
    )(x, norm_w.reshape(1, k), w)


def _matmul_residual_kernel(a_ref, w_ref, r_ref, o_ref):
    o_ref[...] = r_ref[...] + jnp.dot(a_ref[...], w_ref[...], preferred_element_type=F32)


def _matmul_residual(a, w, res):
    t, k = a.shape
    n = w.shape[1]
    tm = _tile(t, TM_MATMUL, V7X_SUBLANES)
    tn = _tile(n, TN_MATMUL // 2, V7X_LANES)
    vmem = _vmem_limit([((tm, k), BF16), ((k, tn), BF16), ((tm, tn), F32), ((tm, tn), F32)])
    return pl.pallas_call(
        _matmul_residual_kernel,
        name="matmul_residual",
        grid=(t // tm, n // tn),
        in_specs=[pl.BlockSpec((tm, k), lambda i, j: (i, 0)),
                  pl.BlockSpec((k, tn), lambda i, j: (0, j)),
                  pl.BlockSpec((tm, tn), lambda i, j: (i, j))],
        out_specs=pl.BlockSpec((tm, tn), lambda i, j: (i, j)),
        out_shape=jax.ShapeDtypeStruct((t, n), F32),
        compiler_params=_params(("parallel", "arbitrary"), vmem),
    )(a, w, res)


def _ffn_kernel(h_ref, nw_ref, wg_ref, wu_ref, wd_ref, o_ref, xn_ref, acc_ref):
    j = pl.program_id(1)

    @pl.when(j == 0)
    def _():
        xn_ref[...] = _rmsnorm_rows(h_ref[...], nw_ref[...]).astype(BF16)
        acc_ref[...] = jnp.zeros_like(acc_ref)

    xn = xn_ref[...]
    g = jnp.dot(xn, wg_ref[...], preferred_element_type=F32)
    u = jnp.dot(xn, wu_ref[...], preferred_element_type=F32)
    a = (_silu(g) * u).astype(BF16)
    acc_ref[...] += jnp.dot(a, wd_ref[...], preferred_element_type=F32)

    @pl.when(j == pl.num_programs(1) - 1)
    def _():
        o_ref[...] = h_ref[...] + 0.5 * acc_ref[...]


def _ffn(h, norm_w, w_gate_up, w_down):
    t, d = h.shape
    f = w_down.shape[0]
    tm = _tile(t, TM_FFN, V7X_SUBLANES)
    tf = _tile(f, TF_FFN, V7X_LANES)
    nf = f // tf
    vmem = _vmem_limit([((tm, d), F32), ((d, tf), BF16), ((d, tf), BF16), ((tf, d), BF16), ((tm, d), F32)],
                       [((tm, d), BF16), ((tm, d), F32)])
    return pl.pallas_call(
        _ffn_kernel,
        name="swiglu_ffn",
        grid=(t // tm, nf),
        in_specs=[pl.BlockSpec((tm, d), lambda i, j: (i, 0)),
                  pl.BlockSpec((1, d), lambda i, j: (0, 0)),
                  pl.BlockSpec((d, tf), lambda i, j: (0, j)),
                  pl.BlockSpec((d, tf), lambda i, j: (0, j + nf)),
                  pl.BlockSpec((tf, d), lambda i, j: (j, 0))],
        out_specs=pl.BlockSpec((tm, d), lambda i, j: (i, 0)),
        out_shape=jax.ShapeDtypeStruct((t, d), F32),
        scratch_shapes=[pltpu.VMEM((tm, d), BF16), pltpu.VMEM((tm, d), F32)],
        compiler_params=_params(("parallel", "arbitrary"), vmem),
    )(h, norm_w.reshape(1, d), w_gate_up, w_gate_up, w_down)


def _rmsnorm_kernel(x_ref, w_ref, o_ref):
    o_ref[...] = _rmsnorm_rows(x_ref[...], w_ref[...])


def _rmsnorm(x, w):
    t, d = x.shape
    tm = _tile(t, TM_ELEMWISE * 2, V7X_SUBLANES)
    return pl.pallas_call(
        _rmsnorm_kernel,
        name="final_rmsnorm",
        grid=(t // tm,),
        in_specs=[pl.BlockSpec((tm, d), lambda i: (i, 0)), pl.BlockSpec((1, d), lambda i: (0, 0))],
        out_specs=pl.BlockSpec((tm, d), lambda i: (i, 0)),
        out_shape=jax.ShapeDtypeStruct((t, d), F32),
        compiler_params=_params(("parallel",), _vmem_limit([((tm, d), F32), ((tm, d), F32)])),
    )(x, w.reshape(1, d))


def _halo_specs(ts, cw, n_tblk, where):
    per = ts // V7X_SUBLANES

    def prev_map(*idx):
        b, tb, c = where(*idx)
        return (b, jnp.maximum(tb * per - 1, 0), c)

    def next_map(*idx):
        b, tb, c = where(*idx)
        return (b, jnp.minimum((tb + 1) * per, n_tblk * per - 1), c)

    return [pl.BlockSpec((1, V7X_SUBLANES, cw), prev_map),
            pl.BlockSpec((1, ts, cw), where),
            pl.BlockSpec((1, V7X_SUBLANES, cw), next_map)]


def _conv_taps(prev_ref, x_ref, next_ref, w, xs_ref, ts, tblk, n_tblk):
    first = tblk == 0
    last = tblk == n_tblk - 1
    h = V7X_SUBLANES
    xs_ref[0:h] = jnp.where(first, 0.0, prev_ref[0])
    xs_ref[h:h + ts] = x_ref[0]
    xs_ref[h + ts:2 * h + ts] = jnp.where(last, 0.0, next_ref[0])
    y = None
    for j in range(CONV_WIDTH):
        off = h - CONV_LEFT + j
        term = xs_ref[off:off + ts] * w[j:j + 1]
        y = term if y is None else y + term
    return y


def _dn_conv_kernel(prev_ref, x_ref, next_ref, w_ref, o_ref, xs_ref, *, ts, n_q_blocks, n_qk_blocks):
    c = pl.program_id(2)
    y = _silu(_conv_taps(prev_ref, x_ref, next_ref, w_ref[...], xs_ref, ts, pl.program_id(1), pl.num_programs(1)))

    @pl.when(c < n_qk_blocks)
    def _():
        scale = jnp.where(c < n_q_blocks, HEAD_DIM ** -0.5, 1.0)
        for hh in range(y.shape[1] // HEAD_DIM):
            yh = y[:, hh * HEAD_DIM:(hh + 1) * HEAD_DIM]
            inv = lax.rsqrt(jnp.sum(yh * yh, axis=-1, keepdims=True) + RMS_EPS)
            o_ref[0, :, hh * HEAD_DIM:(hh + 1) * HEAD_DIM] = yh * inv * scale

    @pl.when(c >= n_qk_blocks)
    def _():
        o_ref[0] = y


def _dn_conv(proj, conv_w, key_dim):
    nb, s, _ = proj.shape
    c = conv_w.shape[1]
    ts = _tile(s, TS_CONV, V7X_SUBLANES)
    cw = _tile(key_dim, CW_CONV, HEAD_DIM)
    kern = functools.partial(_dn_conv_kernel, ts=ts, n_q_blocks=key_dim // cw, n_qk_blocks=2 * key_dim // cw)
    vmem = _vmem_limit([((ts, cw), F32), ((ts, cw), F32)], [((ts + 16, cw), F32), ((ts, cw), F32)])
    return pl.pallas_call(
        kern,
        name="dn_conv",
        grid=(nb, s // ts, c // cw),
        in_specs=(_halo_specs(ts, cw, s // ts, lambda b, i, cc: (b, i, cc))
                  + [pl.BlockSpec((CONV_WIDTH, cw), lambda b, i, cc: (0, cc))]),
        out_specs=pl.BlockSpec((1, ts, cw), lambda b, i, cc: (b, i, cc)),
        out_shape=jax.ShapeDtypeStruct((nb, s, c), F32),
        scratch_shapes=[pltpu.VMEM((ts + 2 * V7X_SUBLANES, cw), F32)],
        compiler_params=_params(("parallel", "parallel", "parallel"), vmem),
    )(proj, proj, proj, conv_w)


def _dn_gates_kernel(x_ref, nw_ref, w_ref, alog_ref, dtb_ref, o_ref, *, n_heads):
    xn = _rmsnorm_rows(x_ref[...], nw_ref[...]).astype(BF16)
    ba = jnp.dot(xn, w_ref[...], preferred_element_type=F32)
    lane = lax.broadcasted_iota(jnp.int32, ba.shape, 1)
    is_g = (lane // n_heads) % 2 == 1
    beta = jax.nn.sigmoid(ba)
    g = -jnp.exp(alog_ref[...]) * _softplus(ba + dtb_ref[...])
    o_ref[...] = jnp.where(is_g, g, beta)


def _dn_gates(x, norm_w, w_ba, a_log, dt_bias):
    t, k = x.shape
    n_heads = a_log.shape[-1]
    gl = 4 * n_heads
    zeros = jnp.zeros((2, 1, n_heads), F32)
    alog_l = jnp.concatenate([zeros, a_log.astype(F32)[:, None, :]], axis=1).reshape(1, gl)
    dtb_l = jnp.concatenate([zeros, dt_bias.astype(F32)[:, None, :]], axis=1).reshape(1, gl)
    tm = _tile(t, TM_MATMUL, V7X_SUBLANES)
    vmem = _vmem_limit([((tm, k), F32), ((k, gl), BF16), ((tm, gl), F32)], [((tm, k), BF16)])
    return pl.pallas_call(
        functools.partial(_dn_gates_kernel, n_heads=n_heads),
        name="dn_gates",
        grid=(t // tm,),
        in_specs=[pl.BlockSpec((tm, k), lambda i: (i, 0)),
                  pl.BlockSpec((1, k), lambda i: (0, 0)),
                  pl.BlockSpec((k, gl), lambda i: (0, 0)),
                  pl.BlockSpec((1, gl), lambda i: (0, 0)),
                  pl.BlockSpec((1, gl), lambda i: (0, 0))],
        out_specs=pl.BlockSpec((tm, gl), lambda i: (i, 0)),
        out_shape=jax.ShapeDtypeStruct((t, gl), F32),
        compiler_params=_params(("parallel",), vmem),
    )(x, norm_w.reshape(1, k), w_ba, alog_l, dtb_l)


def _chunk_cumsum(g, reverse):
    c = g.shape[0]
    row = lax.broadcasted_iota(jnp.int32, g.shape, 0)
    s = 1
    while s < c:
        if reverse:
            g = g + jnp.where(row < c - s, pltpu.roll(g, c - s, 0), 0.0)
        else:
            g = g + jnp.where(row >= s, pltpu.roll(g, s, 0), 0.0)
        s *= 2
    return g


def _row_broadcast(col, eye):
    c = col.shape[0]
    ones = jnp.ones((c, c), BF16)
    hi = col.astype(BF16)
    r1 = col - hi.astype(F32)
    mid = r1.astype(BF16)
    lo = (r1 - mid.astype(F32)).astype(BF16)
    out = None
    for part in (hi, mid, lo):
        diag = jnp.where(eye, jnp.broadcast_to(part.astype(F32), (c, c)), 0.0).astype(BF16)
        term = jnp.dot(ones, diag, preferred_element_type=F32)
        out = term if out is None else out + term
    return out


def _unit_triangular_inverse(lmat, eye_f, m0, m1, m2):
    n0 = -(lmat * m0)
    x = n0
    p = eye_f + n0
    for _ in range(3):
        x = _bdot(x, x)
        p = p + _bdot(p, x)
    t = p - _bdot(_bdot(p, lmat * m1), p)
    return t - _bdot(_bdot(t, lmat * m2), t)


def _delta_rule_kernel(q_ref, k_ref, v_ref, gt_ref, o_ref, s_ref, *, reverse, n_chunks, heads, n_heads):
    c = DN_CHUNK
    hg = pl.program_id(1)

    @pl.when(pl.program_id(2) == 0)
    def _():
        s_ref[...] = jnp.zeros_like(s_ref)

    row = lax.broadcasted_iota(jnp.int32, (c, c), 0)
    col = lax.broadcasted_iota(jnp.int32, (c, c), 1)
    eye = row == col
    eye_f = eye.astype(F32)
    incl = (col >= row) if reverse else (col <= row)
    strict = (col > row) if reverse else (col < row)
    same16 = (row // 16) == (col // 16)
    same32 = (row // 32) == (col // 32)
    m0 = (strict & same16).astype(F32)
    m1 = (strict & same32 & jnp.logical_not(same16)).astype(F32)
    m2 = (strict & jnp.logical_not(same32)).astype(F32)
    gl = gt_ref.shape[-1]
    lane = lax.broadcasted_iota(jnp.int32, (c, gl), 1)
    lane_beta0 = 2 * n_heads if reverse else 0
    lane_g0 = lane_beta0 + n_heads
    nt = (((1,), (1,)), ((), ()))
    tn = (((0,), (0,)), ((), ()))

    def pick(tile, idx):
        return jnp.sum(jnp.where(lane == idx, tile, 0.0), axis=1, keepdims=True)

    def chunk_body(ci, carry):
        cc = (n_chunks - 1 - ci) if reverse else ci
        rows = pl.ds(pl.multiple_of(cc * c, c), c)
        gates = gt_ref[0, rows, :]
        gcum = _chunk_cumsum(gates, reverse)
        for hv in range(heads):
            kh = hv // 2
            hv_global = hg * heads + hv
            k = k_ref[0, rows, kh * HEAD_DIM:(kh + 1) * HEAD_DIM]
            q = q_ref[0, rows, kh * HEAD_DIM:(kh + 1) * HEAD_DIM]
            v = v_ref[0, rows, hv * HEAD_DIM:(hv + 1) * HEAD_DIM]
            beta = pick(gates, lane_beta0 + hv_global)
            gc = pick(gcum, lane_g0 + hv_global)
            g_last = gc[0:1] if reverse else gc[c - 1:c]
            eg = jnp.exp(gc)
            decay = jnp.exp(jnp.where(incl, gc - _row_broadcast(gc, eye), -jnp.inf))
            kb = k * beta
            kq = jnp.concatenate([kb, q], axis=0).astype(BF16)
            kk_qk = lax.dot_general(kq, k.astype(BF16), nt, preferred_element_type=F32)
            lmat = jnp.where(strict, kk_qk[:c] * decay, 0.0)
            attn = kk_qk[c:] * decay
            tinv = _unit_triangular_inverse(lmat, eye_f, m0, m1, m2)
            sol = _bdot(tinv, jnp.concatenate([v * beta, kb * eg], axis=1))
            u = sol[:, :HEAD_DIM]
            w = sol[:, HEAD_DIM:]
            state = s_ref[hv]
            ws = _bdot(jnp.concatenate([w, q * eg], axis=0), state)
            v_new = u - ws[:c]
            o_ref[0, rows, hv * HEAD_DIM:(hv + 1) * HEAD_DIM] = ws[c:] + _bdot(attn, v_new)
            k_dec = k * jnp.exp(g_last - gc)
            s_ref[hv] = state * jnp.exp(g_last) + lax.dot_general(
                k_dec.astype(BF16), v_new.astype(BF16), tn, preferred_element_type=F32)
        return carry

    lax.fori_loop(0, n_chunks, chunk_body, 0)


def _delta_rule(qkv, gates, key_dim, n_heads, reverse):
    nb, s, _ = qkv.shape
    gl = gates.shape[-1]
    heads = min(DN_HEADS_PER_STEP, n_heads)
    rep = n_heads // (key_dim // HEAD_DIM)
    assert rep == 2 and heads % rep == 0 and n_heads % heads == 0
    kw = (heads // rep) * HEAD_DIM
    vw = heads * HEAD_DIM
    n_chunks = _tile(s // DN_CHUNK, DN_CHUNKS_PER_STEP, 1)
    ts = n_chunks * DN_CHUNK
    nblk = s // ts
    tmap = (lambda cb: nblk - 1 - cb) if reverse else (lambda cb: cb)
    k_off = key_dim // kw
    v_off = 2 * key_dim // vw
    kern = functools.partial(_delta_rule_kernel, reverse=reverse, n_chunks=n_chunks, heads=heads, n_heads=n_heads)
    vmem = _vmem_limit([((ts, kw), F32), ((ts, kw), F32), ((ts, vw), F32), ((ts, gl), F32), ((ts, vw), F32)],
                       [((heads, HEAD_DIM, HEAD_DIM), F32)])
    return pl.pallas_call(
        kern,
        name="delta_rule_bwd" if reverse else "delta_rule_fwd",
        grid=(nb, n_heads // heads, nblk),
        in_specs=[pl.BlockSpec((1, ts, kw), lambda b, h, cb: (b, tmap(cb), h)),
                  pl.BlockSpec((1, ts, kw), lambda b, h, cb: (b, tmap(cb), k_off + h)),
                  pl.BlockSpec((1, ts, vw), lambda b, h, cb: (b, tmap(cb), v_off + h)),
                  pl.BlockSpec((1, ts, gl), lambda b, h, cb: (b, tmap(cb), 0))],
        out_specs=pl.BlockSpec((1, ts, vw), lambda b, h, cb: (b, tmap(cb), h)),
        out_shape=jax.ShapeDtypeStruct((nb, s, n_heads * HEAD_DIM), F32),
        scratch_shapes=[pltpu.VMEM((heads, HEAD_DIM, HEAD_DIM), F32)],
        compiler_params=_params(("parallel", "parallel", "arbitrary"), vmem),
    )(qkv, qkv, qkv, gates)


def _dn_out_norm_kernel(of_ref, ob_ref, z_ref, nw_ref, y_ref):
    nw = nw_ref[...]
    for hh in range(of_ref.shape[1] // HEAD_DIM):
        sl = slice(hh * HEAD_DIM, (hh + 1) * HEAD_DIM)
        o = of_ref[:, sl] + ob_ref[:, sl]
        y_ref[:, sl] = (_rmsnorm_rows(o, nw) * _silu(z_ref[:, sl])).astype(BF16)


def _dn_out_norm(o_f, o_b, proj, z_col_block, norm_w):
    t, vd = o_f.shape
    tm = _tile(t, TM_ELEMWISE, 2 * V7X_SUBLANES)
    vmem = _vmem_limit([((tm, vd), F32)] * 3 + [((tm, vd), BF16)])
    return pl.pallas_call(
        _dn_out_norm_kernel,
        name="dn_out_norm",
        grid=(t // tm,),
        in_specs=[pl.BlockSpec((tm, vd), lambda i: (i, 0)),
                  pl.BlockSpec((tm, vd), lambda i: (i, 0)),
                  pl.BlockSpec((tm, vd), lambda i: (i, z_col_block)),
                  pl.BlockSpec((1, HEAD_DIM), lambda i: (0, 0))],
        out_specs=pl.BlockSpec((tm, vd), lambda i: (i, 0)),
        out_shape=jax.ShapeDtypeStruct((t, vd), BF16),
        compiler_params=_params(("parallel",), vmem),
    )(o_f, o_b, proj, norm_w.reshape(1, HEAD_DIM))


def _deltanet_mixer(h, norm_w, w_in, conv_w, a_log, dt_bias, out_norm_w, w_out, nb):
    t, d = h.shape
    s = t // nb
    conv_dim = conv_w.shape[1]
    value_dim = w_out.shape[0]
    key_dim = (conv_dim - value_dim) // 2
    n_heads = a_log.shape[-1]
    assert value_dim == n_heads * HEAD_DIM and conv_dim % value_dim == 0
    w_main = w_in[:, :conv_dim + value_dim].astype(BF16)
    w_ba = w_in[:, conv_dim + value_dim:].astype(BF16)
    proj = _norm_matmul(h, norm_w, w_main)
    gates = _dn_gates(h, norm_w, w_ba, a_log, dt_bias)
    qkv = _dn_conv(proj.reshape(nb, s, -1), conv_w.astype(F32), key_dim)
    gates = gates.reshape(nb, s, -1)
    o_f = _delta_rule(qkv, gates, key_dim, n_heads, reverse=False)
    o_b = _delta_rule(qkv, gates, key_dim, n_heads, reverse=True)
    y = _dn_out_norm(o_f.reshape(t, value_dim), o_b.reshape(t, value_dim), proj, conv_dim // value_dim, out_norm_w)
    return _matmul_residual(y, w_out.astype(BF16), h)


def _lru_scan_kernel(prev_ref, x_ref, next_ref, cw_ref, cb_ref, wg_ref, bg_ref, lam_ref, o_ref,
                     xs_ref, a_ref, b_ref, carry_ref, *, ts, reverse):
    step = pl.program_id(2)
    n_tblk = pl.num_programs(2)

    @pl.when(step == 0)
    def _():
        carry_ref[...] = jnp.zeros_like(carry_ref)

    bw = o_ref.shape[-1]
    tblk = (n_tblk - 1 - step) if reverse else step
    xc = _conv_taps(prev_ref, x_ref, next_ref, cw_ref[...], xs_ref, ts, tblk, n_tblk) + cb_ref[...]
    gates = jnp.dot(xc.astype(BF16), wg_ref[0, 0], preferred_element_type=F32) + bg_ref[0]
    r = jax.nn.sigmoid(gates[:, :bw])
    i = jax.nn.sigmoid(gates[:, bw:])
    log_a = -LRU_C * r * _softplus(-lam_ref[0])
    a = jnp.exp(log_a)
    a_ref[...] = a
    b_ref[...] = jnp.sqrt(jnp.tanh(-log_a) * (a * a + 1.0)) * (i * xc)

    g = V7X_SUBLANES
    row = lax.broadcasted_iota(jnp.int32, (g, bw), 0)
    n_groups = ts // g

    def group_body(gi, carry):
        gg = (n_groups - 1 - gi) if reverse else gi
        rows = pl.ds(pl.multiple_of(gg * g, g), g)
        av = a_ref[rows, :]
        bv = b_ref[rows, :]
        s = 1
        while s < g:
            if reverse:
                valid = row < g - s
                a_sh = jnp.where(valid, pltpu.roll(av, g - s, 0), 1.0)
                b_sh = jnp.where(valid, pltpu.roll(bv, g - s, 0), 0.0)
            else:
                valid = row >= s
                a_sh = jnp.where(valid, pltpu.roll(av, s, 0), 1.0)
                b_sh = jnp.where(valid, pltpu.roll(bv, s, 0), 0.0)
            bv = av * b_sh + bv
            av = av * a_sh
            s *= 2
        hv = av * carry + bv
        o_ref[0, rows, :] = hv
        edge = hv[0:1] if reverse else hv[g - 1:g]
        return jnp.broadcast_to(edge, (g, bw))

    carry_ref[...] = lax.fori_loop(0, n_groups, group_body, carry_ref[...])


def _lru_scan(proj, conv_w, conv_b, w_gates, b_gates, lam, reverse):
    nb, s, _ = proj.shape
    w = conv_w.shape[1]
    d = 1 if reverse else 0
    nblocks, bw = w_gates.shape[1], w_gates.shape[2]
    ts = _tile(s, TS_LRU, V7X_SUBLANES)
    nt = s // ts
    where = (lambda b, c, i: (b, nt - 1 - i, c)) if reverse else (lambda b, c, i: (b, i, c))
    in_specs = _halo_specs(ts, bw, nt, where) + [
        pl.BlockSpec((CONV_WIDTH, bw), lambda b, c, i: (0, c)),
        pl.BlockSpec((1, bw), lambda b, c, i: (0, c)),
        pl.BlockSpec((1, 1, bw, 2 * bw), lambda b, c, i: (d, c, 0, 0)),
        pl.BlockSpec((1, 1, 2 * bw), lambda b, c, i: (d * nblocks + c, 0, 0)),
        pl.BlockSpec((1, 1, bw), lambda b, c, i: (d * nblocks + c, 0, 0)),
    ]
    kern = functools.partial(_lru_scan_kernel, ts=ts, reverse=reverse)
    vmem = _vmem_limit([((ts, bw), F32), ((bw, 2 * bw), BF16), ((ts, bw), F32)],
                       [((ts + 16, bw), F32), ((ts, bw), F32), ((ts, bw), F32), ((ts, 4 * bw), F32)])
    return pl.pallas_call(
        kern,
        name="lru_scan_bwd" if reverse else "lru_scan_fwd",
        grid=(nb, nblocks, nt),
        in_specs=in_specs,
        out_specs=pl.BlockSpec((1, ts, bw), where),
        out_shape=jax.ShapeDtypeStruct((nb, s, w), F32),
        scratch_shapes=[pltpu.VMEM((ts + 2 * V7X_SUBLANES, bw), F32), pltpu.VMEM((ts, bw), F32),
                        pltpu.VMEM((ts, bw), F32), pltpu.VMEM((V7X_SUBLANES, bw), F32)],
        compiler_params=_params(("parallel", "parallel", "arbitrary"), vmem),
    )(proj, proj, proj, conv_w, conv_b.reshape(1, w), w_gates,
      b_gates.reshape(2 * nblocks, 1, 2 * bw), lam.reshape(2 * nblocks, 1, bw))


def _lru_out_gate_kernel(hf_ref, hb_ref, gate_ref, y_ref):
    y_ref[...] = ((hf_ref[...] + hb_ref[...]) * _gelu_tanh(gate_ref[...])).astype(BF16)


def _lru_out_gate(h_f, h_b, proj):
    t, w = h_f.shape
    tm = _tile(t, TM_ELEMWISE * 2, 2 * V7X_SUBLANES)
    vmem = _vmem_limit([((tm, w), F32)] * 3 + [((tm, w), BF16)])
    return pl.pallas_call(
        _lru_out_gate_kernel,
        name="lru_out_gate",
        grid=(t // tm,),
        in_specs=[pl.BlockSpec((tm, w), lambda i: (i, 0)),
                  pl.BlockSpec((tm, w), lambda i: (i, 0)),
                  pl.BlockSpec((tm, w), lambda i: (i, 1))],
        out_specs=pl.BlockSpec((tm, w), lambda i: (i, 0)),
        out_shape=jax.ShapeDtypeStruct((t, w), BF16),
        compiler_params=_params(("parallel",), vmem),
    )(h_f, h_b, proj)


def _rglru_mixer(h, norm_w, w_in, conv_w, conv_b, w_gate_a, b_gate_a, w_gate_x, b_gate_x, lam, w_out, nb):
    t, d = h.shape
    s = t // nb
    w = conv_w.shape[1]
    nblocks, bw = w_gate_a.shape[1], w_gate_a.shape[2]
    proj = _norm_matmul(h, norm_w, w_in.astype(BF16))
    w_gates = jnp.concatenate([w_gate_a, w_gate_x], axis=-1).astype(BF16)
    b_gates = jnp.concatenate([b_gate_a.astype(F32).reshape(2, nblocks, bw),
                               b_gate_x.astype(F32).reshape(2, nblocks, bw)], axis=-1)
    proj3 = proj.reshape(nb, s, 2 * w)
    args = (proj3, conv_w.astype(F32), conv_b.astype(F32), w_gates, b_gates, lam.astype(F32))
    h_f = _lru_scan(*args, reverse=False)
    h_b = _lru_scan(*args, reverse=True)
    y = _lru_out_gate(h_f.reshape(t, w), h_b.reshape(t, w), proj)
    return _matmul_residual(y, w_out.astype(BF16), h)


def kernel(x, ffn1_norm, ffn1_w_gate_up, ffn1_w_down, mix_norm, ffn2_norm, ffn2_w_gate_up, ffn2_w_down,
           dn_w_in, dn_conv_w, dn_a_log, dn_dt_bias, dn_out_norm, dn_w_out,
           lru_w_in, lru_conv_w, lru_conv_b, lru_w_gate_a, lru_b_gate_a, lru_w_gate_x, lru_b_gate_x,
           lru_lambda, lru_w_out, final_norm):
    nb, s, d = x.shape
    h = x.reshape(nb * s, d)
    for layer in range(ffn1_norm.shape[0]):
        h = _ffn(h, ffn1_norm[layer], ffn1_w_gate_up[layer].astype(BF16), ffn1_w_down[layer].astype(BF16))
        j = layer // 2
        if layer % 2 == 0:
            h = _deltanet_mixer(h, mix_norm[layer], dn_w_in[j], dn_conv_w[j], dn_a_log[j], dn_dt_bias[j],
                                dn_out_norm[j], dn_w_out[j], nb)
        else:
            h = _rglru_mixer(h, mix_norm[layer], lru_w_in[j], lru_conv_w[j], lru_conv_b[j], lru_w_gate_a[j],
                             lru_b_gate_a[j], lru_w_gate_x[j], lru_b_gate_x[j], lru_lambda[j], lru_w_out[j], nb)
        h = _ffn(h, ffn2_norm[layer], ffn2_w_gate_up[layer].astype(BF16), ffn2_w_down[layer].astype(BF16))
    return _rmsnorm(h, final_norm).reshape(nb, s, d)
```

```python
import functools
import math

import jax
import jax.numpy as jnp
from jax import lax
from jax.experimental import pallas as pl
from jax.experimental.pallas import tpu as pltpu

F32 = jnp.float32
BF16 = jnp.bfloat16

RMS_EPS = 1e-6
CONV_WIDTH = 4
CONV_LEFT = 2
DN_CHUNK = 64
HEAD_DIM = 128
LRU_C = 8.0

V7X_LANES = 128
V7X_SUBLANES = 8
V7X_VMEM_BYTES = 64 * 1024 * 1024
V7X_VMEM_USABLE = V7X_VMEM_BYTES - 6 * 1024 * 1024

TM_MATMUL = 1024
TN_MATMUL = 1024
TM_FFN = 512
TF_FFN = 512
TS_CONV = 512
CW_CONV = 512
TS_LRU = 256
LRU_SCAN_LANES = 512
LRU_SCAN_UNROLL = 4
DN_HEADS_PER_STEP = 8
DN_CHUNKS_PER_STEP = 8
DN_PREP_GROUP = 2
TM_ELEMWISE = 256


def _tile(extent, preferred, quantum):
    if extent <= preferred:
        return extent
    t = (preferred // quantum) * quantum
    while t >= quantum:
        if extent % t == 0:
            return t
        t -= quantum
    return extent


def _vmem_limit(pipelined, scratch=()):
    nbytes = lambda s, d: math.prod(s) * jnp.dtype(d).itemsize
    total = 2 * sum(nbytes(s, d) for s, d in pipelined) + sum(nbytes(s, d) for s, d in scratch)
    return int(min(V7X_VMEM_USABLE, max(total * 5 // 4, 16 * 1024 * 1024)))


def _params(semantics, vmem):
    return pltpu.CompilerParams(dimension_semantics=semantics, vmem_limit_bytes=vmem)


def _bdot(a, b):
    return jnp.dot(a.astype(BF16), b.astype(BF16), preferred_element_type=F32)


def _rmsnorm_rows(x, w):
    return x * lax.rsqrt(jnp.mean(x * x, axis=-1, keepdims=True) + RMS_EPS) * w


def _softplus(x):
    return jnp.maximum(x, 0.0) + jnp.log1p(jnp.exp(-jnp.abs(x)))


def _silu(x):
    return x * jax.nn.sigmoid(x)


def _gelu_tanh(x):
    return 0.5 * x * (1.0 + jnp.tanh(math.sqrt(2.0 / math.pi) * (x + 0.044715 * (x * x * x))))


def _norm_matmul_kernel(x_ref, nw_ref, w_ref, o_ref, xn_ref):
    @pl.when(pl.program_id(1) == 0)
    def _():
        xn_ref[...] = _rmsnorm_rows(x_ref[...], nw_ref[...]).astype(BF16)

    o_ref[...] = jnp.dot(xn_ref[...], w_ref[...], preferred_element_type=F32)


def _norm_matmul(x, norm_w, w):
    t, k = x.shape
    n = w.shape[1]
    tm = _tile(t, TM_MATMUL, V7X_SUBLANES)
    tn = _tile(n, TN_MATMUL, V7X_LANES)
    vmem = _vmem_limit([((tm, k), F32), ((k, tn), BF16), ((tm, tn), F32)], [((tm, k), BF16)])
    return pl.pallas_call(
        _norm_matmul_kernel,
        name="norm_matmul",
        grid=(t // tm, n // tn),
        in_specs=[pl.BlockSpec((tm, k), lambda i, j: (i, 0)),
                  pl.BlockSpec((1, k), lambda i, j: (0, 0)),
                  pl.BlockSpec((k, tn), lambda i, j: (0, j))],
        out_specs=pl.BlockSpec((tm, tn), lambda i, j: (i, j)),
        out_shape=jax.ShapeDtypeStruct((t, n), F32),
        scratch_shapes=[pltpu.VMEM((tm, k), BF16)],
        compiler_params=_params(("parallel", "arbitrary"), vmem),
    )(x, norm_w.reshape(1, k), w)


def _matmul_residual_kernel(a_ref, w_ref, r_ref, o_ref):
    o_ref[...] = r_ref[...] + jnp.dot(a_ref[...], w_ref[...], preferred_element_type=F32)


def _matmul_residual(a, w, res):
    t, k = a.shape
    n = w.shape[1]
    tm = _tile(t, TM_MATMUL, V7X_SUBLANES)
    tn = _tile(n, TN_MATMUL // 2, V7X_LANES)
    vmem = _vmem_limit([((tm, k), BF16), ((k, tn), BF16), ((tm, tn), F32), ((tm, tn), F32)])
    return pl.pallas_call(
        _matmul_residual_kernel,
        name="matmul_residual",
        grid=(t // tm, n // tn),
        in_specs=[pl.BlockSpec((tm, k), lambda i, j: (i, 0)),
                  pl.BlockSpec((k, tn), lambda i, j: (0, j)),
                  pl.BlockSpec((tm, tn), lambda i, j: (i, j))],
        out_specs=pl.BlockSpec((tm, tn), lambda i, j: (i, j)),
        out_shape=jax.ShapeDtypeStruct((t, n), F32),
        compiler_params=_params(("parallel", "arbitrary"), vmem),
    )(a, w, res)


def _ffn_kernel(h_ref, nw_ref, wg_ref, wu_ref, wd_ref, o_ref, xn_ref, acc_ref):
    j = pl.program_id(1)

    @pl.when(j == 0)
    def _():
        xn_ref[...] = _rmsnorm_rows(h_ref[...], nw_ref[...]).astype(BF16)
        acc_ref[...] = jnp.zeros_like(acc_ref)

    xn = xn_ref[...]
    g = jnp.dot(xn, wg_ref[...], preferred_element_type=F32)
    u = jnp.dot(xn, wu_ref[...], preferred_element_type=F32)
    a = (_silu(g) * u).astype(BF16)
    acc_ref[...] += jnp.dot(a, wd_ref[...], preferred_element_type=F32)

    @pl.when(j == pl.num_programs(1) - 1)
    def _():
        o_ref[...] = h_ref[...] + 0.5 * acc_ref[...]


def _ffn(h, norm_w, w_gate_up, w_down):
    t, d = h.shape
    f = w_down.shape[0]
    tm = _tile(t, TM_FFN, V7X_SUBLANES)
    tf = _tile(f, TF_FFN, V7X_LANES)
    nf = f // tf
    vmem = _vmem_limit([((tm, d), F32), ((d, tf), BF16), ((d, tf), BF16), ((tf, d), BF16), ((tm, d), F32)],
                       [((tm, d), BF16), ((tm, d), F32)])
    return pl.pallas_call(
        _ffn_kernel,
        name="swiglu_ffn",
        grid=(t // tm, nf),
        in_specs=[pl.BlockSpec((tm, d), lambda i, j: (i, 0)),
                  pl.BlockSpec((1, d), lambda i, j: (0, 0)),
                  pl.BlockSpec((d, tf), lambda i, j: (0, j)),
                  pl.BlockSpec((d, tf), lambda i, j: (0, j + nf)),
                  pl.BlockSpec((tf, d), lambda i, j: (j, 0))],
        out_specs=pl.BlockSpec((tm, d), lambda i, j: (i, 0)),
        out_shape=jax.ShapeDtypeStruct((t, d), F32),
        scratch_shapes=[pltpu.VMEM((tm, d), BF16), pltpu.VMEM((tm, d), F32)],
        compiler_params=_params(("parallel", "arbitrary"), vmem),
    )(h, norm_w.reshape(1, d), w_gate_up, w_gate_up, w_down)


def _rmsnorm_kernel(x_ref, w_ref, o_ref):
    o_ref[...] = _rmsnorm_rows(x_ref[...], w_ref[...])


def _rmsnorm(x, w):
    t, d = x.shape
    tm = _tile(t, TM_ELEMWISE * 2, V7X_SUBLANES)
    return pl.pallas_call(
        _rmsnorm_kernel,
        name="final_rmsnorm",
        grid=(t // tm,),
        in_specs=[pl.BlockSpec((tm, d), lambda i: (i, 0)), pl.BlockSpec((1, d), lambda i: (0, 0))],
        out_specs=pl.BlockSpec((tm, d), lambda i: (i, 0)),
        out_shape=jax.ShapeDtypeStruct((t, d), F32),
        compiler_params=_params(("parallel",), _vmem_limit([((tm, d), F32), ((tm, d), F32)])),
    )(x, w.reshape(1, d))


def _halo_specs(ts, cw, n_tblk, where):
    per = ts // V7X_SUBLANES

    def prev_map(*idx):
        b, tb, c = where(*idx)
        return (b, jnp.maximum(tb * per - 1, 0), c)

    def next_map(*idx):
        b, tb, c = where(*idx)
        return (b, jnp.minimum((tb + 1) * per, n_tblk * per - 1), c)

    return [pl.BlockSpec((1, V7X_SUBLANES, cw), prev_map),
            pl.BlockSpec((1, ts, cw), where),
            pl.BlockSpec((1, V7X_SUBLANES, cw), next_map)]


def _conv_taps(prev_ref, x_ref, next_ref, w, xs_ref, ts, tblk, n_tblk):
    first = tblk == 0
    last = tblk == n_tblk - 1
    h = V7X_SUBLANES
    xs_ref[0:h] = jnp.where(first, 0.0, prev_ref[0])
    xs_ref[h:h + ts] = x_ref[0]
    xs_ref[h + ts:2 * h + ts] = jnp.where(last, 0.0, next_ref[0])
    y = None
    for j in range(CONV_WIDTH):
        off = h - CONV_LEFT + j
        term = xs_ref[off:off + ts] * w[j:j + 1]
        y = term if y is None else y + term
    return y


def _dn_conv_kernel(prev_ref, x_ref, next_ref, w_ref, o_ref, xs_ref, *, ts, n_q_blocks, n_qk_blocks):
    c = pl.program_id(2)
    y = _silu(_conv_taps(prev_ref, x_ref, next_ref, w_ref[...], xs_ref, ts, pl.program_id(1), pl.num_programs(1)))

    @pl.when(c < n_qk_blocks)
    def _():
        scale = jnp.where(c < n_q_blocks, HEAD_DIM ** -0.5, 1.0)
        for hh in range(y.shape[1] // HEAD_DIM):
            yh = y[:, hh * HEAD_DIM:(hh + 1) * HEAD_DIM]
            inv = lax.rsqrt(jnp.sum(yh * yh, axis=-1, keepdims=True) + RMS_EPS)
            o_ref[0, :, hh * HEAD_DIM:(hh + 1) * HEAD_DIM] = yh * inv * scale

    @pl.when(c >= n_qk_blocks)
    def _():
        o_ref[0] = y


def _dn_conv(proj, conv_w, key_dim):
    nb, s, _ = proj.shape
    c = conv_w.shape[1]
    ts = _tile(s, TS_CONV, V7X_SUBLANES)
    cw = _tile(key_dim, CW_CONV, HEAD_DIM)
    kern = functools.partial(_dn_conv_kernel, ts=ts, n_q_blocks=key_dim // cw, n_qk_blocks=2 * key_dim // cw)
    vmem = _vmem_limit([((ts, cw), F32), ((ts, cw), F32)], [((ts + 16, cw), F32), ((ts, cw), F32)])
    return pl.pallas_call(
        kern,
        name="dn_conv",
        grid=(nb, s // ts, c // cw),
        in_specs=(_halo_specs(ts, cw, s // ts, lambda b, i, cc: (b, i, cc))
                  + [pl.BlockSpec((CONV_WIDTH, cw), lambda b, i, cc: (0, cc))]),
        out_specs=pl.BlockSpec((1, ts, cw), lambda b, i, cc: (b, i, cc)),
        out_shape=jax.ShapeDtypeStruct((nb, s, c), F32),
        scratch_shapes=[pltpu.VMEM((ts + 2 * V7X_SUBLANES, cw), F32)],
        compiler_params=_params(("parallel", "parallel", "parallel"), vmem),
    )(proj, proj, proj, conv_w)


def _dn_gates_kernel(x_ref, nw_ref, w_ref, alog_ref, dtb_ref, o_ref, *, n_heads):
    xn = _rmsnorm_rows(x_ref[...], nw_ref[...]).astype(BF16)
    ba = jnp.dot(xn, w_ref[...], preferred_element_type=F32)
    lane = lax.broadcasted_iota(jnp.int32, ba.shape, 1)
    is_g = (lane // n_heads) % 2 == 1
    beta = jax.nn.sigmoid(ba)
    g = -jnp.exp(alog_ref[...]) * _softplus(ba + dtb_ref[...])
    o_ref[...] = jnp.where(is_g, g, beta)


def _dn_gates(x, norm_w, w_ba, a_log, dt_bias):
    t, k = x.shape
    n_heads = a_log.shape[-1]
    gl = 4 * n_heads
    zeros = jnp.zeros((2, 1, n_heads), F32)
    alog_l = jnp.concatenate([zeros, a_log.astype(F32)[:, None, :]], axis=1).reshape(1, gl)
    dtb_l = jnp.concatenate([zeros, dt_bias.astype(F32)[:, None, :]], axis=1).reshape(1, gl)
    tm = _tile(t, TM_MATMUL, V7X_SUBLANES)
    vmem = _vmem_limit([((tm, k), F32), ((k, gl), BF16), ((tm, gl), F32)], [((tm, k), BF16)])
    return pl.pallas_call(
        functools.partial(_dn_gates_kernel, n_heads=n_heads),
        name="dn_gates",
        grid=(t // tm,),
        in_specs=[pl.BlockSpec((tm, k), lambda i: (i, 0)),
                  pl.BlockSpec((1, k), lambda i: (0, 0)),
                  pl.BlockSpec((k, gl), lambda i: (0, 0)),
                  pl.BlockSpec((1, gl), lambda i: (0, 0)),
                  pl.BlockSpec((1, gl), lambda i: (0, 0))],
        out_specs=pl.BlockSpec((tm, gl), lambda i: (i, 0)),
        out_shape=jax.ShapeDtypeStruct((t, gl), F32),
        compiler_params=_params(("parallel",), vmem),
    )(x, norm_w.reshape(1, k), w_ba, alog_l, dtb_l)


def _chunk_cumsum(g, reverse):
    c = g.shape[0]
    row = lax.broadcasted_iota(jnp.int32, g.shape, 0)
    s = 1
    while s < c:
        if reverse:
            g = g + jnp.where(row < c - s, pltpu.roll(g, c - s, 0), 0.0)
        else:
            g = g + jnp.where(row >= s, pltpu.roll(g, s, 0), 0.0)
        s *= 2
    return g


def _square_pad(x, p):
    c, l = x.shape
    if l < p:
        x = jnp.concatenate([x, jnp.zeros((c, p - l), x.dtype)], axis=1)
    if c < p:
        x = jnp.concatenate([x] * (p // c), axis=0)
    return x


def _delta_rule_kernel(q_ref, k_ref, v_ref, gt_ref, o_ref,
                       s_ref, u_ref, wq_ref, attn_ref, kdec_ref, egl_ref, gct_ref,
                       *, reverse, n_chunks, group, heads, n_heads):
    c = DN_CHUNK
    hg = pl.program_id(1)

    @pl.when(pl.program_id(2) == 0)
    def _():
        s_ref[...] = jnp.zeros_like(s_ref)

    row = lax.broadcasted_iota(jnp.int32, (c, c), 0)
    col = lax.broadcasted_iota(jnp.int32, (c, c), 1)
    eye = row == col
    eye_f = eye.astype(F32)
    incl = (col >= row) if reverse else (col <= row)
    strict = (col > row) if reverse else (col < row)
    same16 = (row // 16) == (col // 16)
    same32 = (row // 32) == (col // 32)
    m0 = (strict & same16).astype(F32)
    m1 = (strict & same32 & jnp.logical_not(same16)).astype(F32)
    m2 = (strict & jnp.logical_not(same32)).astype(F32)
    gl = gt_ref.shape[-1]
    lane = lax.broadcasted_iota(jnp.int32, (c, gl), 1)
    lane_beta0 = 2 * n_heads if reverse else 0
    lane_g0 = lane_beta0 + n_heads
    nt = (((1,), (1,)), ((), ()))
    tn = (((0,), (0,)), ((), ()))

    pad = gct_ref.shape[-1]
    rep = 2
    hd = HEAD_DIM

    def pick(tile, idx):
        return jnp.sum(jnp.where(lane == idx, tile, 0.0), axis=1, keepdims=True)

    def prep_body(gi, carry):
        rows_l, gates_l = [], []
        for ci in range(group):
            rows = pl.ds(pl.multiple_of((gi * group + ci) * c, c), c)
            gates = gt_ref[0, rows, :]
            gcum = _chunk_cumsum(gates, reverse)
            gct_ref[ci] = _square_pad(gcum, pad).T
            rows_l.append(rows)
            gates_l.append((gates, gcum))
        keys = [(ci, kh) for ci in range(group) for kh in range(heads // rep)]
        insts = [(ci, hv) for ci in range(group) for hv in range(heads)]
        k_d = {key: k_ref[0, rows_l[key[0]], key[1] * hd:(key[1] + 1) * hd] for key in keys}
        q_d = {key: q_ref[0, rows_l[key[0]], key[1] * hd:(key[1] + 1) * hd] for key in keys}
        beta_d, gc_d, decay_d, kb_d = {}, {}, {}, {}
        for ci, hv in insts:
            gates, gcum = gates_l[ci]
            hv_global = hg * heads + hv
            beta_d[ci, hv] = pick(gates, lane_beta0 + hv_global)
            gc = pick(gcum, lane_g0 + hv_global)
            g_row = gct_ref[ci, pl.ds(lane_g0 + hv_global, 1), :][:, :c]
            gc_d[ci, hv] = gc
            decay_d[ci, hv] = jnp.exp(jnp.where(incl, gc - g_row, -jnp.inf))
            kb_d[ci, hv] = k_d[ci, hv // rep] * beta_d[ci, hv]
        kq_d = {}
        for key in keys:
            ci, kh = key
            lhs = jnp.concatenate([kb_d[ci, kh * rep + r] for r in range(rep)] + [q_d[key]], axis=0)
            kq_d[key] = lax.dot_general(lhs.astype(BF16), k_d[key].astype(BF16), nt, preferred_element_type=F32)
        lm_d, x_d, p_d = {}, {}, {}
        for inst in insts:
            ci, hv = inst
            kh, r = hv // rep, hv % rep
            kk = kq_d[ci, kh][r * c:(r + 1) * c]
            qk = kq_d[ci, kh][rep * c:]
            lm_d[inst] = jnp.where(strict, kk * decay_d[inst], 0.0)
            attn_ref[gi * group + ci, hv] = (qk * decay_d[inst]).astype(BF16)
            n0 = -(lm_d[inst] * m0)
            x_d[inst] = n0
            p_d[inst] = eye_f + n0
        for _ in range(3):
            for inst in insts:
                x_d[inst] = _bdot(x_d[inst], x_d[inst])
            for inst in insts:
                p_d[inst] = p_d[inst] + _bdot(p_d[inst], x_d[inst])
        for mask in (m1, m2):
            y_d = {inst: _bdot(p_d[inst], lm_d[inst] * mask) for inst in insts}
            for inst in insts:
                p_d[inst] = p_d[inst] - _bdot(y_d[inst], p_d[inst])
        sol_d = {}
        for inst in insts:
            ci, hv = inst
            eg = jnp.exp(gc_d[inst])
            v = v_ref[0, rows_l[ci], hv * hd:(hv + 1) * hd]
            rhs = jnp.concatenate([v * beta_d[inst], kb_d[inst] * eg], axis=1)
            sol_d[inst] = _bdot(p_d[inst], rhs)
        for inst in insts:
            ci, hv = inst
            cc = gi * group + ci
            gc = gc_d[inst]
            g_last = gc[0:1] if reverse else gc[c - 1:c]
            eg = jnp.exp(gc)
            u_ref[cc, hv] = sol_d[inst][:, :hd]
            wq_ref[cc, hv] = jnp.concatenate([sol_d[inst][:, hd:], q_d[ci, hv // rep] * eg], axis=0).astype(BF16)
            kdec_ref[cc, hv] = (k_d[ci, hv // rep] * jnp.exp(g_last - gc)).astype(BF16)
            egl_ref[cc, hv] = jnp.broadcast_to(jnp.exp(g_last), egl_ref.shape[2:])
        return carry

    lax.fori_loop(0, n_chunks // group, prep_body, 0)

    def rec_body(ci, carry):
        cc = (n_chunks - 1 - ci) if reverse else ci
        rows = pl.ds(pl.multiple_of(cc * c, c), c)
        hs = range(heads)
        states = [s_ref[h] for h in hs]
        ws = [jnp.dot(wq_ref[cc, h], states[h].astype(BF16), preferred_element_type=F32) for h in hs]
        v_new = [(u_ref[cc, h] - ws[h][:c]).astype(BF16) for h in hs]
        av = [jnp.dot(attn_ref[cc, h], v_new[h], preferred_element_type=F32) for h in hs]
        kv = [lax.dot_general(kdec_ref[cc, h], v_new[h], tn, preferred_element_type=F32) for h in hs]
        for h in hs:
            o_ref[0, rows, h * hd:(h + 1) * hd] = ws[h][c:] + av[h]
            s_ref[h] = states[h] * egl_ref[cc, h][0:1, :] + kv[h]
        return carry

    lax.fori_loop(0, n_chunks, rec_body, 0)


def _delta_rule(qkv, gates, key_dim, n_heads, reverse):
    nb, s, _ = qkv.shape
    gl = gates.shape[-1]
    heads = min(DN_HEADS_PER_STEP, n_heads)
    rep = n_heads // (key_dim // HEAD_DIM)
    assert rep == 2 and heads % rep == 0 and n_heads % heads == 0
    kw = (heads // rep) * HEAD_DIM
    vw = heads * HEAD_DIM
    n_chunks = _tile(s // DN_CHUNK, DN_CHUNKS_PER_STEP, 1)
    ts = n_chunks * DN_CHUNK
    nblk = s // ts
    tmap = (lambda cb: nblk - 1 - cb) if reverse else (lambda cb: cb)
    k_off = key_dim // kw
    v_off = 2 * key_dim // vw
    group = math.gcd(DN_PREP_GROUP, n_chunks)
    pad = max(DN_CHUNK, gl)
    assert pad % DN_CHUNK == 0 and pad % gl == 0
    kern = functools.partial(_delta_rule_kernel, reverse=reverse, n_chunks=n_chunks, group=group, heads=heads,
                             n_heads=n_heads)
    c = DN_CHUNK
    scratch = [((heads, HEAD_DIM, HEAD_DIM), F32),
               ((n_chunks, heads, c, HEAD_DIM), F32),
               ((n_chunks, heads, 2 * c, HEAD_DIM), BF16),
               ((n_chunks, heads, c, c), BF16),
               ((n_chunks, heads, c, HEAD_DIM), BF16),
               ((n_chunks, heads, V7X_SUBLANES, V7X_LANES), F32),
               ((group, pad, pad), F32)]
    vmem = _vmem_limit([((ts, kw), F32), ((ts, kw), F32), ((ts, vw), F32), ((ts, gl), F32), ((ts, vw), F32)],
                       scratch + [((n_chunks, heads, c, V7X_LANES), BF16)])
    return pl.pallas_call(
        kern,
        name="delta_rule_bwd" if reverse else "delta_rule_fwd",
        grid=(nb, n_heads // heads, nblk),
        in_specs=[pl.BlockSpec((1, ts, kw), lambda b, h, cb: (b, tmap(cb), h)),
                  pl.BlockSpec((1, ts, kw), lambda b, h, cb: (b, tmap(cb), k_off + h)),
                  pl.BlockSpec((1, ts, vw), lambda b, h, cb: (b, tmap(cb), v_off + h)),
                  pl.BlockSpec((1, ts, gl), lambda b, h, cb: (b, tmap(cb), 0))],
        out_specs=pl.BlockSpec((1, ts, vw), lambda b, h, cb: (b, tmap(cb), h)),
        out_shape=jax.ShapeDtypeStruct((nb, s, n_heads * HEAD_DIM), F32),
        scratch_shapes=[pltpu.VMEM(s_, d_) for s_, d_ in scratch],
        compiler_params=_params(("parallel", "parallel", "arbitrary"), vmem),
    )(qkv, qkv, qkv, gates)


def _dn_out_norm_kernel(of_ref, ob_ref, z_ref, nw_ref, y_ref):
    nw = nw_ref[...]
    for hh in range(of_ref.shape[1] // HEAD_DIM):
        sl = slice(hh * HEAD_DIM, (hh + 1) * HEAD_DIM)
        o = of_ref[:, sl] + ob_ref[:, sl]
        y_ref[:, sl] = (_rmsnorm_rows(o, nw) * _silu(z_ref[:, sl])).astype(BF16)


def _dn_out_norm(o_f, o_b, proj, z_col_block, norm_w):
    t, vd = o_f.shape
    tm = _tile(t, TM_ELEMWISE, 2 * V7X_SUBLANES)
    vmem = _vmem_limit([((tm, vd), F32)] * 3 + [((tm, vd), BF16)])
    return pl.pallas_call(
        _dn_out_norm_kernel,
        name="dn_out_norm",
        grid=(t // tm,),
        in_specs=[pl.BlockSpec((tm, vd), lambda i: (i, 0)),
                  pl.BlockSpec((tm, vd), lambda i: (i, 0)),
                  pl.BlockSpec((tm, vd), lambda i: (i, z_col_block)),
                  pl.BlockSpec((1, HEAD_DIM), lambda i: (0, 0))],
        out_specs=pl.BlockSpec((tm, vd), lambda i: (i, 0)),
        out_shape=jax.ShapeDtypeStruct((t, vd), BF16),
        compiler_params=_params(("parallel",), vmem),
    )(o_f, o_b, proj, norm_w.reshape(1, HEAD_DIM))


def _deltanet_mixer(h, norm_w, w_in, conv_w, a_log, dt_bias, out_norm_w, w_out, nb):
    t, d = h.shape
    s = t // nb
    conv_dim = conv_w.shape[1]
    value_dim = w_out.shape[0]
    key_dim = (conv_dim - value_dim) // 2
    n_heads = a_log.shape[-1]
    assert value_dim == n_heads * HEAD_DIM and conv_dim % value_dim == 0
    w_main = w_in[:, :conv_dim + value_dim].astype(BF16)
    w_ba = w_in[:, conv_dim + value_dim:].astype(BF16)
    proj = _norm_matmul(h, norm_w, w_main)
    gates = _dn_gates(h, norm_w, w_ba, a_log, dt_bias)
    qkv = _dn_conv(proj.reshape(nb, s, -1), conv_w.astype(F32), key_dim)
    gates = gates.reshape(nb, s, -1)
    o_f = _delta_rule(qkv, gates, key_dim, n_heads, reverse=False)
    o_b = _delta_rule(qkv, gates, key_dim, n_heads, reverse=True)
    y = _dn_out_norm(o_f.reshape(t, value_dim), o_b.reshape(t, value_dim), proj, conv_dim // value_dim, out_norm_w)
    return _matmul_residual(y, w_out.astype(BF16), h)


def _lru_scan_kernel(prev_ref, x_ref, next_ref, cw_ref, cb_ref, wg_ref, bg_ref, lam_ref, o_ref,
                     xs_ref, a_ref, b_ref, carry_ref, *, ts, reverse):
    step = pl.program_id(1)
    n_tblk = pl.num_programs(1)

    @pl.when(step == 0)
    def _():
        carry_ref[...] = jnp.zeros_like(carry_ref)

    nblocks, bw = wg_ref.shape[1], wg_ref.shape[2]
    tblk = (n_tblk - 1 - step) if reverse else step
    xs = _conv_taps(prev_ref, x_ref, next_ref, cw_ref[...], xs_ref, ts, tblk, n_tblk) + cb_ref[...]
    for blk in range(nblocks):
        cols = slice(blk * bw, (blk + 1) * bw)
        xc = xs[:, cols]
        gates = jnp.dot(xc.astype(BF16), wg_ref[0, blk], preferred_element_type=F32) + bg_ref[0, blk]
        r = jax.nn.sigmoid(gates[:, :bw])
        i = jax.nn.sigmoid(gates[:, bw:])
        log_a = -LRU_C * r * _softplus(-lam_ref[0, :, cols])
        a = jnp.exp(log_a)
        a_ref[:, cols] = a
        b_ref[:, cols] = jnp.sqrt(jnp.tanh(-log_a) * (a * a + 1.0)) * (i * xc)

    g = V7X_SUBLANES
    n_groups = ts // g
    width = a_ref.shape[1]
    cwid = math.gcd(width, LRU_SCAN_LANES)
    row = lax.broadcasted_iota(jnp.int32, (g, cwid), 0)
    for cchunk in range(width // cwid):
        cols = slice(cchunk * cwid, (cchunk + 1) * cwid)

        def group_body(gi, carry, cols=cols):
            gg = (n_groups - 1 - gi) if reverse else gi
            rows = pl.ds(pl.multiple_of(gg * g, g), g)
            av = a_ref[rows, cols]
            bv = b_ref[rows, cols]
            s = 1
            while s < g:
                if reverse:
                    valid = row < g - s
                    a_sh = jnp.where(valid, pltpu.roll(av, g - s, 0), 1.0)
                    b_sh = jnp.where(valid, pltpu.roll(bv, g - s, 0), 0.0)
                else:
                    valid = row >= s
                    a_sh = jnp.where(valid, pltpu.roll(av, s, 0), 1.0)
                    b_sh = jnp.where(valid, pltpu.roll(bv, s, 0), 0.0)
                bv = av * b_sh + bv
                av = av * a_sh
                s *= 2
            hv = av * carry + bv
            o_ref[0, rows, cols] = hv
            edge = hv[0:1] if reverse else hv[g - 1:g]
            return jnp.broadcast_to(edge, (g, cwid))

        carry_ref[:, cols] = lax.fori_loop(0, n_groups, group_body, carry_ref[:, cols],
                                           unroll=math.gcd(n_groups, LRU_SCAN_UNROLL))


def _lru_scan(proj, conv_w, conv_b, w_gates, b_gates, lam, reverse):
    nb, s, _ = proj.shape
    w = conv_w.shape[1]
    d = 1 if reverse else 0
    nblocks, bw = w_gates.shape[1], w_gates.shape[2]
    ts = _tile(s, TS_LRU, V7X_SUBLANES)
    nt = s // ts
    where = (lambda b, i: (b, nt - 1 - i, 0)) if reverse else (lambda b, i: (b, i, 0))
    in_specs = _halo_specs(ts, w, nt, where) + [
        pl.BlockSpec((CONV_WIDTH, w), lambda b, i: (0, 0)),
        pl.BlockSpec((1, w), lambda b, i: (0, 0)),
        pl.BlockSpec((1, nblocks, bw, 2 * bw), lambda b, i: (d, 0, 0, 0)),
        pl.BlockSpec((1, nblocks, 1, 2 * bw), lambda b, i: (d, 0, 0, 0)),
        pl.BlockSpec((1, 1, w), lambda b, i: (d, 0, 0)),
    ]
    kern = functools.partial(_lru_scan_kernel, ts=ts, reverse=reverse)
    vmem = _vmem_limit([((ts, w), F32), ((nblocks, bw, 2 * bw), BF16), ((ts, w), F32)],
                       [((ts + 16, w), F32), ((ts, w), F32), ((ts, w), F32), ((ts, w), F32)])
    return pl.pallas_call(
        kern,
        name="lru_scan_bwd" if reverse else "lru_scan_fwd",
        grid=(nb, nt),
        in_specs=in_specs,
        out_specs=pl.BlockSpec((1, ts, w), where),
        out_shape=jax.ShapeDtypeStruct((nb, s, w), F32),
        scratch_shapes=[pltpu.VMEM((ts + 2 * V7X_SUBLANES, w), F32), pltpu.VMEM((ts, w), F32),
                        pltpu.VMEM((ts, w), F32), pltpu.VMEM((V7X_SUBLANES, w), F32)],
        compiler_params=_params(("parallel", "arbitrary"), vmem),
    )(proj, proj, proj, conv_w, conv_b.reshape(1, w), w_gates, b_gates, lam.reshape(2, 1, w))


def _lru_out_gate_kernel(hf_ref, hb_ref, gate_ref, y_ref):
    y_ref[...] = ((hf_ref[...] + hb_ref[...]) * _gelu_tanh(gate_ref[...])).astype(BF16)


def _lru_out_gate(h_f, h_b, proj):
    t, w = h_f.shape
    tm = _tile(t, TM_ELEMWISE * 2, 2 * V7X_SUBLANES)
    vmem = _vmem_limit([((tm, w), F32)] * 3 + [((tm, w), BF16)])
    return pl.pallas_call(
        _lru_out_gate_kernel,
        name="lru_out_gate",
        grid=(t // tm,),
        in_specs=[pl.BlockSpec((tm, w), lambda i: (i, 0)),
                  pl.BlockSpec((tm, w), lambda i: (i, 0)),
                  pl.BlockSpec((tm, w), lambda i: (i, 1))],
        out_specs=pl.BlockSpec((tm, w), lambda i: (i, 0)),
        out_shape=jax.ShapeDtypeStruct((t, w), BF16),
        compiler_params=_params(("parallel",), vmem),
    )(h_f, h_b, proj)


def _rglru_mixer(h, norm_w, w_in, conv_w, conv_b, w_gate_a, b_gate_a, w_gate_x, b_gate_x, lam, w_out, nb):
    t, d = h.shape
    s = t // nb
    w = conv_w.shape[1]
    nblocks, bw = w_gate_a.shape[1], w_gate_a.shape[2]
    proj = _norm_matmul(h, norm_w, w_in.astype(BF16))
    w_gates = jnp.concatenate([w_gate_a, w_gate_x], axis=-1).astype(BF16)
    b_gates = jnp.concatenate([b_gate_a.astype(F32).reshape(2, nblocks, bw),
                               b_gate_x.astype(F32).reshape(2, nblocks, bw)], axis=-1)[:, :, None, :]
    proj3 = proj.reshape(nb, s, 2 * w)
    args = (proj3, conv_w.astype(F32), conv_b.astype(F32), w_gates, b_gates, lam.astype(F32))
    h_f = _lru_scan(*args, reverse=False)
    h_b = _lru_scan(*args, reverse=True)
    y = _lru_out_gate(h_f.reshape(t, w), h_b.reshape(t, w), proj)
    return _matmul_residual(y, w_out.astype(BF16), h)


def kernel(x, ffn1_norm, ffn1_w_gate_up, ffn1_w_down, mix_norm, ffn2_norm, ffn2_w_gate_up, ffn2_w_down,
           dn_w_in, dn_conv_w, dn_a_log, dn_dt_bias, dn_out_norm, dn_w_out,
           lru_w_in, lru_conv_w, lru_conv_b, lru_w_gate_a, lru_b_gate_a, lru_w_gate_x, lru_b_gate_x,
           lru_lambda, lru_w_out, final_norm):
    nb, s, d = x.shape
    h = x.reshape(nb * s, d)
    for layer in range(ffn1_norm.shape[0]):
        h = _ffn(h, ffn1_norm[layer], ffn1_w_gate_up[layer].astype(BF16), ffn1_w_down[layer].astype(BF16))
        j = layer // 2
        if layer % 2 == 0:
            h = _deltanet_mixer(h, mix_norm[layer], dn_w_in[j], dn_conv_w[j], dn_a_log[j], dn_dt_bias[j],
                                dn_out_norm[j], dn_w_out[j], nb)
        else:
            h = _rglru_mixer(h, mix_norm[layer], lru_w_in[j], lru_conv_w[j], lru_conv_b[j], lru_w_gate_a[j],
                             lru_b_gate_a[j], lru_w_gate_x[j], lru_b_gate_x[j], lru_lambda[j], lru_w_out[j], nb)
        h = _ffn(h, ffn2_norm[layer], ffn2_w_gate_up[layer].astype(BF16), ffn2_w_down[layer].astype(BF16))
    return _rmsnorm(h, final_norm).reshape(nb, s, d)
```

```python
import functools
import math

import jax
import jax.numpy as jnp
from jax import lax
from jax.experimental import pallas as pl
from jax.experimental.pallas import tpu as pltpu

F32 = jnp.float32
BF16 = jnp.bfloat16

RMS_EPS = 1e-6
CONV_WIDTH = 4
CONV_LEFT = 2
DN_CHUNK = 64
HEAD_DIM = 128
LRU_C = 8.0

V7X_LANES = 128
V7X_SUBLANES = 8
V7X_VMEM_BYTES = 64 * 1024 * 1024
V7X_VMEM_USABLE = V7X_VMEM_BYTES - 6 * 1024 * 1024

TM_MATMUL = 1024
TN_MATMUL = 1024
TM_FFN = 512
TF_FFN = 512
TS_CONV = 128
CW_CONV = 2048
TS_LRU = 256
LRU_SCAN_LANES = 512
LRU_SCAN_UNROLL = 4
DN_HEADS_PER_STEP = 16
DN_CHUNKS_PER_STEP = 4
DN_PREP_GROUP = 2
TM_ELEMWISE = 256


def _tile(extent, preferred, quantum):
    if extent <= preferred:
        return extent
    t = (preferred // quantum) * quantum
    while t >= quantum:
        if extent % t == 0:
            return t
        t -= quantum
    return extent


def _vmem_limit(pipelined, scratch=()):
    nbytes = lambda s, d: math.prod(s) * jnp.dtype(d).itemsize
    total = 2 * sum(nbytes(s, d) for s, d in pipelined) + sum(nbytes(s, d) for s, d in scratch)
    return int(min(V7X_VMEM_USABLE, max(total * 5 // 4, 16 * 1024 * 1024)))


def _params(semantics, vmem):
    return pltpu.CompilerParams(dimension_semantics=semantics, vmem_limit_bytes=vmem)


def _bdot(a, b):
    return jnp.dot(a.astype(BF16), b.astype(BF16), preferred_element_type=F32)


def _rmsnorm_rows(x, w):
    return x * lax.rsqrt(jnp.mean(x * x, axis=-1, keepdims=True) + RMS_EPS) * w


def _softplus(x):
    return jnp.maximum(x, 0.0) + jnp.log1p(jnp.exp(-jnp.abs(x)))


def _silu(x):
    return x * jax.nn.sigmoid(x)


def _gelu_tanh(x):
    return 0.5 * x * (1.0 + jnp.tanh(math.sqrt(2.0 / math.pi) * (x + 0.044715 * (x * x * x))))


def _norm_matmul_kernel(x_ref, nw_ref, w_ref, o_ref, xn_ref):
    @pl.when(pl.program_id(1) == 0)
    def _():
        xn_ref[...] = _rmsnorm_rows(x_ref[...], nw_ref[...]).astype(BF16)

    o_ref[...] = jnp.dot(xn_ref[...], w_ref[...], preferred_element_type=F32)


def _norm_matmul(x, norm_w, w):
    t, k = x.shape
    n = w.shape[1]
    tm = _tile(t, TM_MATMUL, V7X_SUBLANES)
    tn = _tile(n, TN_MATMUL, V7X_LANES)
    vmem = _vmem_limit([((tm, k), F32), ((k, tn), BF16), ((tm, tn), F32)], [((tm, k), BF16)])
    return pl.pallas_call(
        _norm_matmul_kernel,
        name="norm_matmul",
        grid=(t // tm, n // tn),
        in_specs=[pl.BlockSpec((tm, k), lambda i, j: (i, 0)),
                  pl.BlockSpec((1, k), lambda i, j: (0, 0)),
                  pl.BlockSpec((k, tn), lambda i, j: (0, j))],
        out_specs=pl.BlockSpec((tm, tn), lambda i, j: (i, j)),
        out_shape=jax.ShapeDtypeStruct((t, n), F32),
        scratch_shapes=[pltpu.VMEM((tm, k), BF16)],
        compiler_params=_params(("parallel", "arbitrary"), vmem),
    )(x, norm_w.reshape(1, k), w)


def _matmul_residual_kernel(a_ref, w_ref, r_ref, o_ref):
    o_ref[...] = r_ref[...] + jnp.dot(a_ref[...], w_ref[...], preferred_element_type=F32)


def _matmul_residual(a, w, res):
    t, k = a.shape
    n = w.shape[1]
    tm = _tile(t, TM_MATMUL, V7X_SUBLANES)
    tn = _tile(n, TN_MATMUL // 2, V7X_LANES)
    vmem = _vmem_limit([((tm, k), BF16), ((k, tn), BF16), ((tm, tn), F32), ((tm, tn), F32)])
    return pl.pallas_call(
        _matmul_residual_kernel,
        name="matmul_residual",
        grid=(t // tm, n // tn),
        in_specs=[pl.BlockSpec((tm, k), lambda i, j: (i, 0)),
                  pl.BlockSpec((k, tn), lambda i, j: (0, j)),
                  pl.BlockSpec((tm, tn), lambda i, j: (i, j))],
        out_specs=pl.BlockSpec((tm, tn), lambda i, j: (i, j)),
        out_shape=jax.ShapeDtypeStruct((t, n), F32),
        compiler_params=_params(("parallel", "arbitrary"), vmem),
    )(a, w, res)


def _ffn_kernel(h_ref, nw_ref, wg_ref, wu_ref, wd_ref, o_ref, xn_ref, acc_ref):
    j = pl.program_id(1)

    @pl.when(j == 0)
    def _():
        xn_ref[...] = _rmsnorm_rows(h_ref[...], nw_ref[...]).astype(BF16)
        acc_ref[...] = jnp.zeros_like(acc_ref)

    xn = xn_ref[...]
    g = jnp.dot(xn, wg_ref[...], preferred_element_type=F32)
    u = jnp.dot(xn, wu_ref[...], preferred_element_type=F32)
    a = (_silu(g) * u).astype(BF16)
    acc_ref[...] += jnp.dot(a, wd_ref[...], preferred_element_type=F32)

    @pl.when(j == pl.num_programs(1) - 1)
    def _():
        o_ref[...] = h_ref[...] + 0.5 * acc_ref[...]


def _ffn(h, norm_w, w_gate_up, w_down):
    t, d = h.shape
    f = w_down.shape[0]
    tm = _tile(t, TM_FFN, V7X_SUBLANES)
    tf = _tile(f, TF_FFN, V7X_LANES)
    nf = f // tf
    vmem = _vmem_limit([((tm, d), F32), ((d, tf), BF16), ((d, tf), BF16), ((tf, d), BF16), ((tm, d), F32)],
                       [((tm, d), BF16), ((tm, d), F32)])
    return pl.pallas_call(
        _ffn_kernel,
        name="swiglu_ffn",
        grid=(t // tm, nf),
        in_specs=[pl.BlockSpec((tm, d), lambda i, j: (i, 0)),
                  pl.BlockSpec((1, d), lambda i, j: (0, 0)),
                  pl.BlockSpec((d, tf), lambda i, j: (0, j)),
                  pl.BlockSpec((d, tf), lambda i, j: (0, j + nf)),
                  pl.BlockSpec((tf, d), lambda i, j: (j, 0))],
        out_specs=pl.BlockSpec((tm, d), lambda i, j: (i, 0)),
        out_shape=jax.ShapeDtypeStruct((t, d), F32),
        scratch_shapes=[pltpu.VMEM((tm, d), BF16), pltpu.VMEM((tm, d), F32)],
        compiler_params=_params(("parallel", "arbitrary"), vmem),
    )(h, norm_w.reshape(1, d), w_gate_up, w_gate_up, w_down)


def _rmsnorm_kernel(x_ref, w_ref, o_ref):
    o_ref[...] = _rmsnorm_rows(x_ref[...], w_ref[...])


def _rmsnorm(x, w):
    t, d = x.shape
    tm = _tile(t, TM_ELEMWISE * 2, V7X_SUBLANES)
    return pl.pallas_call(
        _rmsnorm_kernel,
        name="final_rmsnorm",
        grid=(t // tm,),
        in_specs=[pl.BlockSpec((tm, d), lambda i: (i, 0)), pl.BlockSpec((1, d), lambda i: (0, 0))],
        out_specs=pl.BlockSpec((tm, d), lambda i: (i, 0)),
        out_shape=jax.ShapeDtypeStruct((t, d), F32),
        compiler_params=_params(("parallel",), _vmem_limit([((tm, d), F32), ((tm, d), F32)])),
    )(x, w.reshape(1, d))


def _halo_specs(ts, cw, n_tblk, where):
    per = ts // V7X_SUBLANES

    def prev_map(*idx):
        b, tb, c = where(*idx)
        return (b, jnp.maximum(tb * per - 1, 0), c)

    def next_map(*idx):
        b, tb, c = where(*idx)
        return (b, jnp.minimum((tb + 1) * per, n_tblk * per - 1), c)

    return [pl.BlockSpec((1, V7X_SUBLANES, cw), prev_map),
            pl.BlockSpec((1, ts, cw), where),
            pl.BlockSpec((1, V7X_SUBLANES, cw), next_map)]


def _conv_taps(prev_ref, x_ref, next_ref, w, xs_ref, ts, tblk, n_tblk):
    first = tblk == 0
    last = tblk == n_tblk - 1
    h = V7X_SUBLANES
    xs_ref[0:h] = jnp.where(first, 0.0, prev_ref[0])
    xs_ref[h:h + ts] = x_ref[0]
    xs_ref[h + ts:2 * h + ts] = jnp.where(last, 0.0, next_ref[0])
    y = None
    for j in range(CONV_WIDTH):
        off = h - CONV_LEFT + j
        term = xs_ref[off:off + ts] * w[j:j + 1]
        y = term if y is None else y + term
    return y


def _dn_conv_kernel(prev_ref, x_ref, next_ref, w_ref, o_ref, xs_ref, *, ts, n_q_blocks, n_qk_blocks):
    c = pl.program_id(2)
    y = _silu(_conv_taps(prev_ref, x_ref, next_ref, w_ref[...], xs_ref, ts, pl.program_id(1), pl.num_programs(1)))

    @pl.when(c < n_qk_blocks)
    def _():
        scale = jnp.where(c < n_q_blocks, HEAD_DIM ** -0.5, 1.0)
        for hh in range(y.shape[1] // HEAD_DIM):
            yh = y[:, hh * HEAD_DIM:(hh + 1) * HEAD_DIM]
            inv = lax.rsqrt(jnp.sum(yh * yh, axis=-1, keepdims=True) + RMS_EPS)
            o_ref[0, :, hh * HEAD_DIM:(hh + 1) * HEAD_DIM] = yh * inv * scale

    @pl.when(c >= n_qk_blocks)
    def _():
        o_ref[0] = y


def _dn_conv(proj, conv_w, key_dim):
    nb, s, _ = proj.shape
    c = conv_w.shape[1]
    ts = _tile(s, TS_CONV, V7X_SUBLANES)
    cw = _tile(key_dim, CW_CONV, HEAD_DIM)
    kern = functools.partial(_dn_conv_kernel, ts=ts, n_q_blocks=key_dim // cw, n_qk_blocks=2 * key_dim // cw)
    vmem = _vmem_limit([((ts, cw), F32), ((ts, cw), F32)], [((ts + 16, cw), F32), ((ts, cw), F32)])
    return pl.pallas_call(
        kern,
        name="dn_conv",
        grid=(nb, s // ts, c // cw),
        in_specs=(_halo_specs(ts, cw, s // ts, lambda b, i, cc: (b, i, cc))
                  + [pl.BlockSpec((CONV_WIDTH, cw), lambda b, i, cc: (0, cc))]),
        out_specs=pl.BlockSpec((1, ts, cw), lambda b, i, cc: (b, i, cc)),
        out_shape=jax.ShapeDtypeStruct((nb, s, c), F32),
        scratch_shapes=[pltpu.VMEM((ts + 2 * V7X_SUBLANES, cw), F32)],
        compiler_params=_params(("parallel", "parallel", "parallel"), vmem),
    )(proj, proj, proj, conv_w)


def _dn_gates_kernel(x_ref, nw_ref, w_ref, alog_ref, dtb_ref, o_ref, *, n_heads):
    xn = _rmsnorm_rows(x_ref[...], nw_ref[...]).astype(BF16)
    ba = jnp.dot(xn, w_ref[...], preferred_element_type=F32)
    lane = lax.broadcasted_iota(jnp.int32, ba.shape, 1)
    is_g = (lane // n_heads) % 2 == 1
    beta = jax.nn.sigmoid(ba)
    g = -jnp.exp(alog_ref[...]) * _softplus(ba + dtb_ref[...])
    o_ref[...] = jnp.where(is_g, g, beta)


def _dn_gates(x, norm_w, w_ba, a_log, dt_bias):
    t, k = x.shape
    n_heads = a_log.shape[-1]
    gl = 4 * n_heads
    zeros = jnp.zeros((2, 1, n_heads), F32)
    alog_l = jnp.concatenate([zeros, a_log.astype(F32)[:, None, :]], axis=1).reshape(1, gl)
    dtb_l = jnp.concatenate([zeros, dt_bias.astype(F32)[:, None, :]], axis=1).reshape(1, gl)
    tm = _tile(t, TM_MATMUL, V7X_SUBLANES)
    vmem = _vmem_limit([((tm, k), F32), ((k, gl), BF16), ((tm, gl), F32)], [((tm, k), BF16)])
    return pl.pallas_call(
        functools.partial(_dn_gates_kernel, n_heads=n_heads),
        name="dn_gates",
        grid=(t // tm,),
        in_specs=[pl.BlockSpec((tm, k), lambda i: (i, 0)),
                  pl.BlockSpec((1, k), lambda i: (0, 0)),
                  pl.BlockSpec((k, gl), lambda i: (0, 0)),
                  pl.BlockSpec((1, gl), lambda i: (0, 0)),
                  pl.BlockSpec((1, gl), lambda i: (0, 0))],
        out_specs=pl.BlockSpec((tm, gl), lambda i: (i, 0)),
        out_shape=jax.ShapeDtypeStruct((t, gl), F32),
        compiler_params=_params(("parallel",), vmem),
    )(x, norm_w.reshape(1, k), w_ba, alog_l, dtb_l)


def _chunk_cumsum(g, reverse):
    c = g.shape[0]
    row = lax.broadcasted_iota(jnp.int32, g.shape, 0)
    s = 1
    while s < c:
        if reverse:
            g = g + jnp.where(row < c - s, pltpu.roll(g, c - s, 0), 0.0)
        else:
            g = g + jnp.where(row >= s, pltpu.roll(g, s, 0), 0.0)
        s *= 2
    return g


def _square_pad(x, p):
    c, l = x.shape
    if l < p:
        x = jnp.concatenate([x, jnp.zeros((c, p - l), x.dtype)], axis=1)
    if c < p:
        x = jnp.concatenate([x] * (p // c), axis=0)
    return x


def _delta_rule_kernel(q_ref, k_ref, v_ref, gt_ref, o_ref,
                       s_ref, u_ref, wq_ref, attn_ref, kdec_ref, egl_ref, gct_ref,
                       *, reverse, n_chunks, group, heads, n_heads):
    c = DN_CHUNK
    hg = pl.program_id(1)

    @pl.when(pl.program_id(2) == 0)
    def _():
        s_ref[...] = jnp.zeros_like(s_ref)

    row = lax.broadcasted_iota(jnp.int32, (c, c), 0)
    col = lax.broadcasted_iota(jnp.int32, (c, c), 1)
    eye = row == col
    eye_f = eye.astype(F32)
    incl = (col >= row) if reverse else (col <= row)
    strict = (col > row) if reverse else (col < row)
    same16 = (row // 16) == (col // 16)
    same32 = (row // 32) == (col // 32)
    m0 = (strict & same16).astype(F32)
    m1 = (strict & same32 & jnp.logical_not(same16)).astype(F32)
    m2 = (strict & jnp.logical_not(same32)).astype(F32)
    gl = gt_ref.shape[-1]
    lane = lax.broadcasted_iota(jnp.int32, (c, gl), 1)
    lane_beta0 = 2 * n_heads if reverse else 0
    lane_g0 = lane_beta0 + n_heads
    nt = (((1,), (1,)), ((), ()))
    tn = (((0,), (0,)), ((), ()))

    pad = gct_ref.shape[-1]
    rep = 2
    hd = HEAD_DIM

    def pick(tile, idx):
        return jnp.sum(jnp.where(lane == idx, tile, 0.0), axis=1, keepdims=True)

    def prep_body(gi, carry):
        rows_l, gates_l = [], []
        for ci in range(group):
            rows = pl.ds(pl.multiple_of((gi * group + ci) * c, c), c)
            gates = gt_ref[0, rows, :]
            gcum = _chunk_cumsum(gates, reverse)
            gct_ref[ci] = _square_pad(gcum, pad).T
            rows_l.append(rows)
            gates_l.append((gates, gcum))
        keys = [(ci, kh) for ci in range(group) for kh in range(heads // rep)]
        insts = [(ci, hv) for ci in range(group) for hv in range(heads)]
        k_d = {key: k_ref[0, rows_l[key[0]], key[1] * hd:(key[1] + 1) * hd] for key in keys}
        q_d = {key: q_ref[0, rows_l[key[0]], key[1] * hd:(key[1] + 1) * hd] for key in keys}
        beta_d, gc_d, decay_d, kb_d = {}, {}, {}, {}
        for ci, hv in insts:
            gates, gcum = gates_l[ci]
            hv_global = hg * heads + hv
            beta_d[ci, hv] = pick(gates, lane_beta0 + hv_global)
            gc = pick(gcum, lane_g0 + hv_global)
            g_row = gct_ref[ci, pl.ds(lane_g0 + hv_global, 1), :][:, :c]
            gc_d[ci, hv] = gc
            decay_d[ci, hv] = jnp.exp(jnp.where(incl, gc - g_row, -jnp.inf))
            kb_d[ci, hv] = k_d[ci, hv // rep] * beta_d[ci, hv]
        kq_d = {}
        for key in keys:
            ci, kh = key
            lhs = jnp.concatenate([kb_d[ci, kh * rep + r] for r in range(rep)] + [q_d[key]], axis=0)
            kq_d[key] = lax.dot_general(lhs.astype(BF16), k_d[key].astype(BF16), nt, preferred_element_type=F32)
        lm_d, x_d, p_d = {}, {}, {}
        for inst in insts:
            ci, hv = inst
            kh, r = hv // rep, hv % rep
            kk = kq_d[ci, kh][r * c:(r + 1) * c]
            qk = kq_d[ci, kh][rep * c:]
            lm_d[inst] = jnp.where(strict, kk * decay_d[inst], 0.0)
            attn_ref[gi * group + ci, hv] = (qk * decay_d[inst]).astype(BF16)
            n0 = -(lm_d[inst] * m0)
            x_d[inst] = n0
            p_d[inst] = eye_f + n0
        for _ in range(3):
            for inst in insts:
                x_d[inst] = _bdot(x_d[inst], x_d[inst])
            for inst in insts:
                p_d[inst] = p_d[inst] + _bdot(p_d[inst], x_d[inst])
        for mask in (m1, m2):
            y_d = {inst: _bdot(p_d[inst], lm_d[inst] * mask) for inst in insts}
            for inst in insts:
                p_d[inst] = p_d[inst] - _bdot(y_d[inst], p_d[inst])
        sol_d = {}
        for inst in insts:
            ci, hv = inst
            eg = jnp.exp(gc_d[inst])
            v = v_ref[0, rows_l[ci], hv * hd:(hv + 1) * hd]
            rhs = jnp.concatenate([v * beta_d[inst], kb_d[inst] * eg], axis=1)
            sol_d[inst] = _bdot(p_d[inst], rhs)
        for inst in insts:
            ci, hv = inst
            cc = gi * group + ci
            gc = gc_d[inst]
            g_last = gc[0:1] if reverse else gc[c - 1:c]
            eg = jnp.exp(gc)
            u_ref[cc, hv] = sol_d[inst][:, :hd]
            wq_ref[cc, hv] = jnp.concatenate([sol_d[inst][:, hd:], q_d[ci, hv // rep] * eg], axis=0).astype(BF16)
            kdec_ref[cc, hv] = (k_d[ci, hv // rep] * jnp.exp(g_last - gc)).astype(BF16)
            egl_ref[cc, hv] = jnp.broadcast_to(jnp.exp(g_last), egl_ref.shape[2:])
        return carry

    lax.fori_loop(0, n_chunks // group, prep_body, 0)

    def rec_body(ci, carry):
        cc = (n_chunks - 1 - ci) if reverse else ci
        rows = pl.ds(pl.multiple_of(cc * c, c), c)
        hs = range(heads)
        states = [s_ref[h] for h in hs]
        ws = [jnp.dot(wq_ref[cc, h], states[h].astype(BF16), preferred_element_type=F32) for h in hs]
        v_new = [(u_ref[cc, h] - ws[h][:c]).astype(BF16) for h in hs]
        av = [jnp.dot(attn_ref[cc, h], v_new[h], preferred_element_type=F32) for h in hs]
        kv = [lax.dot_general(kdec_ref[cc, h], v_new[h], tn, preferred_element_type=F32) for h in hs]
        for h in hs:
            o_ref[0, rows, h * hd:(h + 1) * hd] = ws[h][c:] + av[h]
            s_ref[h] = states[h] * egl_ref[cc, h][0:1, :] + kv[h]
        return carry

    lax.fori_loop(0, n_chunks, rec_body, 0)


def _delta_rule(qkv, gates, key_dim, n_heads, reverse):
    nb, s, _ = qkv.shape
    gl = gates.shape[-1]
    heads = min(DN_HEADS_PER_STEP, n_heads)
    rep = n_heads // (key_dim // HEAD_DIM)
    assert rep == 2 and heads % rep == 0 and n_heads % heads == 0
    kw = (heads // rep) * HEAD_DIM
    vw = heads * HEAD_DIM
    n_chunks = _tile(s // DN_CHUNK, DN_CHUNKS_PER_STEP, 1)
    ts = n_chunks * DN_CHUNK
    nblk = s // ts
    tmap = (lambda cb: nblk - 1 - cb) if reverse else (lambda cb: cb)
    k_off = key_dim // kw
    v_off = 2 * key_dim // vw
    group = math.gcd(DN_PREP_GROUP, n_chunks)
    pad = max(DN_CHUNK, gl)
    assert pad % DN_CHUNK == 0 and pad % gl == 0
    kern = functools.partial(_delta_rule_kernel, reverse=reverse, n_chunks=n_chunks, group=group, heads=heads,
                             n_heads=n_heads)
    c = DN_CHUNK
    scratch = [((heads, HEAD_DIM, HEAD_DIM), F32),
               ((n_chunks, heads, c, HEAD_DIM), F32),
               ((n_chunks, heads, 2 * c, HEAD_DIM), BF16),
               ((n_chunks, heads, c, c), BF16),
               ((n_chunks, heads, c, HEAD_DIM), BF16),
               ((n_chunks, heads, V7X_SUBLANES, V7X_LANES), F32),
               ((group, pad, pad), F32)]
    vmem = _vmem_limit([((ts, kw), F32), ((ts, kw), F32), ((ts, vw), F32), ((ts, gl), F32), ((ts, vw), F32)],
                       scratch + [((n_chunks, heads, c, V7X_LANES), BF16),
                                  ((group * heads, 16 * c, HEAD_DIM), F32)])
    return pl.pallas_call(
        kern,
        name="delta_rule_bwd" if reverse else "delta_rule_fwd",
        grid=(nb, n_heads // heads, nblk),
        in_specs=[pl.BlockSpec((1, ts, kw), lambda b, h, cb: (b, tmap(cb), h)),
                  pl.BlockSpec((1, ts, kw), lambda b, h, cb: (b, tmap(cb), k_off + h)),
                  pl.BlockSpec((1, ts, vw), lambda b, h, cb: (b, tmap(cb), v_off + h)),
                  pl.BlockSpec((1, ts, gl), lambda b, h, cb: (b, tmap(cb), 0))],
        out_specs=pl.BlockSpec((1, ts, vw), lambda b, h, cb: (b, tmap(cb), h)),
        out_shape=jax.ShapeDtypeStruct((nb, s, n_heads * HEAD_DIM), F32),
        scratch_shapes=[pltpu.VMEM(s_, d_) for s_, d_ in scratch],
        compiler_params=_params(("parallel", "parallel", "arbitrary"), vmem),
    )(qkv, qkv, qkv, gates)


def _dn_out_norm_kernel(of_ref, ob_ref, z_ref, nw_ref, y_ref):
    nw = nw_ref[...]
    for hh in range(of_ref.shape[1] // HEAD_DIM):
        sl = slice(hh * HEAD_DIM, (hh + 1) * HEAD_DIM)
        o = of_ref[:, sl] + ob_ref[:, sl]
        y_ref[:, sl] = (_rmsnorm_rows(o, nw) * _silu(z_ref[:, sl])).astype(BF16)


def _dn_out_norm(o_f, o_b, proj, z_col_block, norm_w):
    t, vd = o_f.shape
    tm = _tile(t, TM_ELEMWISE, 2 * V7X_SUBLANES)
    vmem = _vmem_limit([((tm, vd), F32)] * 3 + [((tm, vd), BF16)])
    return pl.pallas_call(
        _dn_out_norm_kernel,
        name="dn_out_norm",
        grid=(t // tm,),
        in_specs=[pl.BlockSpec((tm, vd), lambda i: (i, 0)),
                  pl.BlockSpec((tm, vd), lambda i: (i, 0)),
                  pl.BlockSpec((tm, vd), lambda i: (i, z_col_block)),
                  pl.BlockSpec((1, HEAD_DIM), lambda i: (0, 0))],
        out_specs=pl.BlockSpec((tm, vd), lambda i: (i, 0)),
        out_shape=jax.ShapeDtypeStruct((t, vd), BF16),
        compiler_params=_params(("parallel",), vmem),
    )(o_f, o_b, proj, norm_w.reshape(1, HEAD_DIM))


def _deltanet_mixer(h, norm_w, w_in, conv_w, a_log, dt_bias, out_norm_w, w_out, nb):
    t, d = h.shape
    s = t // nb
    conv_dim = conv_w.shape[1]
    value_dim = w_out.shape[0]
    key_dim = (conv_dim - value_dim) // 2
    n_heads = a_log.shape[-1]
    assert value_dim == n_heads * HEAD_DIM and conv_dim % value_dim == 0
    w_main = w_in[:, :conv_dim + value_dim].astype(BF16)
    w_ba = w_in[:, conv_dim + value_dim:].astype(BF16)
    proj = _norm_matmul(h, norm_w, w_main)
    gates = _dn_gates(h, norm_w, w_ba, a_log, dt_bias)
    qkv = _dn_conv(proj.reshape(nb, s, -1), conv_w.astype(F32), key_dim)
    gates = gates.reshape(nb, s, -1)
    o_f = _delta_rule(qkv, gates, key_dim, n_heads, reverse=False)
    o_b = _delta_rule(qkv, gates, key_dim, n_heads, reverse=True)
    y = _dn_out_norm(o_f.reshape(t, value_dim), o_b.reshape(t, value_dim), proj, conv_dim // value_dim, out_norm_w)
    return _matmul_residual(y, w_out.astype(BF16), h)


def _lru_scan_kernel(prev_ref, x_ref, next_ref, cw_ref, cb_ref, wg_ref, bg_ref, lam_ref, o_ref,
                     xs_ref, a_ref, b_ref, carry_ref, *, ts, reverse):
    step = pl.program_id(1)
    n_tblk = pl.num_programs(1)

    @pl.when(step == 0)
    def _():
        carry_ref[...] = jnp.zeros_like(carry_ref)

    nblocks, bw = wg_ref.shape[1], wg_ref.shape[2]
    tblk = (n_tblk - 1 - step) if reverse else step
    xs = _conv_taps(prev_ref, x_ref, next_ref, cw_ref[...], xs_ref, ts, tblk, n_tblk) + cb_ref[...]
    for blk in range(nblocks):
        cols = slice(blk * bw, (blk + 1) * bw)
        xc = xs[:, cols]
        gates = jnp.dot(xc.astype(BF16), wg_ref[0, blk], preferred_element_type=F32) + bg_ref[0, blk]
        r = jax.nn.sigmoid(gates[:, :bw])
        i = jax.nn.sigmoid(gates[:, bw:])
        log_a = -LRU_C * r * _softplus(-lam_ref[0, :, cols])
        a = jnp.exp(log_a)
        a_ref[:, cols] = a
        b_ref[:, cols] = jnp.sqrt(jnp.tanh(-log_a) * (a * a + 1.0)) * (i * xc)

    g = V7X_SUBLANES
    n_groups = ts // g
    width = a_ref.shape[1]
    cwid = math.gcd(width, LRU_SCAN_LANES)
    row = lax.broadcasted_iota(jnp.int32, (g, cwid), 0)
    for cchunk in range(width // cwid):
        cols = slice(cchunk * cwid, (cchunk + 1) * cwid)

        def group_body(gi, carry, cols=cols):
            gg = (n_groups - 1 - gi) if reverse else gi
            rows = pl.ds(pl.multiple_of(gg * g, g), g)
            av = a_ref[rows, cols]
            bv = b_ref[rows, cols]
            s = 1
            while s < g:
                if reverse:
                    valid = row < g - s
                    a_sh = jnp.where(valid, pltpu.roll(av, g - s, 0), 1.0)
                    b_sh = jnp.where(valid, pltpu.roll(bv, g - s, 0), 0.0)
                else:
                    valid = row >= s
                    a_sh = jnp.where(valid, pltpu.roll(av, s, 0), 1.0)
                    b_sh = jnp.where(valid, pltpu.roll(bv, s, 0), 0.0)
                bv = av * b_sh + bv
                av = av * a_sh
                s *= 2
            hv = av * carry + bv
            o_ref[0, rows, cols] = hv
            edge = hv[0:1] if reverse else hv[g - 1:g]
            return jnp.broadcast_to(edge, (g, cwid))

        carry_ref[:, cols] = lax.fori_loop(0, n_groups, group_body, carry_ref[:, cols],
                                           unroll=math.gcd(n_groups, LRU_SCAN_UNROLL))


def _lru_scan(proj, conv_w, conv_b, w_gates, b_gates, lam, reverse):
    nb, s, _ = proj.shape
    w = conv_w.shape[1]
    d = 1 if reverse else 0
    nblocks, bw = w_gates.shape[1], w_gates.shape[2]
    ts = _tile(s, TS_LRU, V7X_SUBLANES)
    nt = s // ts
    where = (lambda b, i: (b, nt - 1 - i, 0)) if reverse else (lambda b, i: (b, i, 0))
    in_specs = _halo_specs(ts, w, nt, where) + [
        pl.BlockSpec((CONV_WIDTH, w), lambda b, i: (0, 0)),
        pl.BlockSpec((1, w), lambda b, i: (0, 0)),
        pl.BlockSpec((1, nblocks, bw, 2 * bw), lambda b, i: (d, 0, 0, 0)),
        pl.BlockSpec((1, nblocks, 1, 2 * bw), lambda b, i: (d, 0, 0, 0)),
        pl.BlockSpec((1, 1, w), lambda b, i: (d, 0, 0)),
    ]
    kern = functools.partial(_lru_scan_kernel, ts=ts, reverse=reverse)
    vmem = _vmem_limit([((ts, w), F32), ((nblocks, bw, 2 * bw), BF16), ((ts, w), F32)],
                       [((ts + 16, w), F32), ((ts, w), F32), ((ts, w), F32), ((ts, w), F32)])
    return pl.pallas_call(
        kern,
        name="lru_scan_bwd" if reverse else "lru_scan_fwd",
        grid=(nb, nt),
        in_specs=in_specs,
        out_specs=pl.BlockSpec((1, ts, w), where),
        out_shape=jax.ShapeDtypeStruct((nb, s, w), F32),
        scratch_shapes=[pltpu.VMEM((ts + 2 * V7X_SUBLANES, w), F32), pltpu.VMEM((ts, w), F32),
                        pltpu.VMEM((ts, w), F32), pltpu.VMEM((V7X_SUBLANES, w), F32)],
        compiler_params=_params(("parallel", "arbitrary"), vmem),
    )(proj, proj, proj, conv_w, conv_b.reshape(1, w), w_gates, b_gates, lam.reshape(2, 1, w))


def _lru_out_gate_kernel(hf_ref, hb_ref, gate_ref, y_ref):
    y_ref[...] = ((hf_ref[...] + hb_ref[...]) * _gelu_tanh(gate_ref[...])).astype(BF16)


def _lru_out_gate(h_f, h_b, proj):
    t, w = h_f.shape
    tm = _tile(t, TM_ELEMWISE * 2, 2 * V7X_SUBLANES)
    vmem = _vmem_limit([((tm, w), F32)] * 3 + [((tm, w), BF16)])
    return pl.pallas_call(
        _lru_out_gate_kernel,
        name="lru_out_gate",
        grid=(t // tm,),
        in_specs=[pl.BlockSpec((tm, w), lambda i: (i, 0)),
                  pl.BlockSpec((tm, w), lambda i: (i, 0)),
                  pl.BlockSpec((tm, w), lambda i: (i, 1))],
        out_specs=pl.BlockSpec((tm, w), lambda i: (i, 0)),
        out_shape=jax.ShapeDtypeStruct((t, w), BF16),
        compiler_params=_params(("parallel",), vmem),
    )(h_f, h_b, proj)


def _rglru_mixer(h, norm_w, w_in, conv_w, conv_b, w_gate_a, b_gate_a, w_gate_x, b_gate_x, lam, w_out, nb):
    t, d = h.shape
    s = t // nb
    w = conv_w.shape[1]
    nblocks, bw = w_gate_a.shape[1], w_gate_a.shape[2]
    proj = _norm_matmul(h, norm_w, w_in.astype(BF16))
    w_gates = jnp.concatenate([w_gate_a, w_gate_x], axis=-1).astype(BF16)
    b_gates = jnp.concatenate([b_gate_a.astype(F32).reshape(2, nblocks, bw),
                               b_gate_x.astype(F32).reshape(2, nblocks, bw)], axis=-1)[:, :, None, :]
    proj3 = proj.reshape(nb, s, 2 * w)
    args = (proj3, conv_w.astype(F32), conv_b.astype(F32), w_gates, b_gates, lam.astype(F32))
    h_f = _lru_scan(*args, reverse=False)
    h_b = _lru_scan(*args, reverse=True)
    y = _lru_out_gate(h_f.reshape(t, w), h_b.reshape(t, w), proj)
    return _matmul_residual(y, w_out.astype(BF16), h)


def kernel(x, ffn1_norm, ffn1_w_gate_up, ffn1_w_down, mix_norm, ffn2_norm, ffn2_w_gate_up, ffn2_w_down,
           dn_w_in, dn_conv_w, dn_a_log, dn_dt_bias, dn_out_norm, dn_w_out,
           lru_w_in, lru_conv_w, lru_conv_b, lru_w_gate_a, lru_b_gate_a, lru_w_gate_x, lru_b_gate_x,
           lru_lambda, lru_w_out, final_norm):
    nb, s, d = x.shape
    h = x.reshape(nb * s, d)
    for layer in range(ffn1_norm.shape[0]):
        h = _ffn(h, ffn1_norm[layer], ffn1_w_gate_up[layer].astype(BF16), ffn1_w_down[layer].astype(BF16))
        j = layer // 2
        if layer % 2 == 0:
            h = _deltanet_mixer(h, mix_norm[layer], dn_w_in[j], dn_conv_w[j], dn_a_log[j], dn_dt_bias[j],
                                dn_out_norm[j], dn_w_out[j], nb)
        else:
            h = _rglru_mixer(h, mix_norm[layer], lru_w_in[j], lru_conv_w[j], lru_conv_b[j], lru_w_gate_a[j],
                             lru_b_gate_a[j], lru_w_gate_x[j], lru_b_gate_x[j], lru_lambda[j], lru_w_out[j], nb)
        h = _ffn(h, ffn2_norm[layer], ffn2_w_gate_up[layer].astype(BF16), ffn2_w_down[layer].astype(BF16))
    return _rmsnorm(h, final_norm).reshape(nb, s, d)
```

```python
import functools
import math

import jax
import jax.numpy as jnp
from jax import lax
from jax.experimental import pallas as pl
from jax.experimental.pallas import tpu as pltpu

F32 = jnp.float32
BF16 = jnp.bfloat16

RMS_EPS = 1e-6
CONV_WIDTH = 4
CONV_LEFT = 2
DN_CHUNK = 64
HEAD_DIM = 128
LRU_C = 8.0

V7X_LANES = 128
V7X_SUBLANES = 8
V7X_VMEM_BYTES = 64 * 1024 * 1024
V7X_VMEM_USABLE = V7X_VMEM_BYTES - 6 * 1024 * 1024

TM_MATMUL = 1024
TN_MATMUL = 1024
TM_FFN = 512
TF_FFN = 512
TS_CONV = 512
CW_CONV = 512
TS_LRU = 256
LRU_SCAN_LANES = 512
LRU_SCAN_UNROLL = 4
DN_HEADS_PER_STEP = 16
DN_CHUNKS_PER_STEP = 4
DN_PREP_GROUP = 4
TM_ELEMWISE = 256


def _tile(extent, preferred, quantum):
    if extent <= preferred:
        return extent
    t = (preferred // quantum) * quantum
    while t >= quantum:
        if extent % t == 0:
            return t
        t -= quantum
    return extent


def _vmem_limit(pipelined, scratch=()):
    nbytes = lambda s, d: math.prod(s) * jnp.dtype(d).itemsize
    total = 2 * sum(nbytes(s, d) for s, d in pipelined) + sum(nbytes(s, d) for s, d in scratch)
    return int(min(V7X_VMEM_USABLE, max(total * 5 // 4, 16 * 1024 * 1024)))


def _params(semantics, vmem):
    return pltpu.CompilerParams(dimension_semantics=semantics, vmem_limit_bytes=vmem)


def _bdot(a, b):
    return jnp.dot(a.astype(BF16), b.astype(BF16), preferred_element_type=F32)


def _rmsnorm_rows(x, w):
    return x * lax.rsqrt(jnp.mean(x * x, axis=-1, keepdims=True) + RMS_EPS) * w


def _softplus(x):
    return jnp.maximum(x, 0.0) + jnp.log1p(jnp.exp(-jnp.abs(x)))


def _silu(x):
    return x * jax.nn.sigmoid(x)


def _gelu_tanh(x):
    return 0.5 * x * (1.0 + jnp.tanh(math.sqrt(2.0 / math.pi) * (x + 0.044715 * (x * x * x))))


def _norm_matmul_kernel(x_ref, nw_ref, w_ref, o_ref, xn_ref):
    @pl.when(pl.program_id(1) == 0)
    def _():
        xn_ref[...] = _rmsnorm_rows(x_ref[...], nw_ref[...]).astype(BF16)

    o_ref[...] = jnp.dot(xn_ref[...], w_ref[...], preferred_element_type=F32)


def _norm_matmul(x, norm_w, w_stack, layer, n):
    t, k = x.shape
    tm = _tile(t, TM_MATMUL, V7X_SUBLANES)
    tn = _tile(n, TN_MATMUL, V7X_LANES)
    vmem = _vmem_limit([((tm, k), F32), ((k, tn), BF16), ((tm, tn), F32)], [((tm, k), BF16)])
    return pl.pallas_call(
        _norm_matmul_kernel,
        name="norm_matmul",
        grid=(t // tm, n // tn),
        in_specs=[pl.BlockSpec((tm, k), lambda i, j: (i, 0)),
                  pl.BlockSpec((1, k), lambda i, j: (0, 0)),
                  pl.BlockSpec((None, k, tn), lambda i, j: (layer, 0, j))],
        out_specs=pl.BlockSpec((tm, tn), lambda i, j: (i, j)),
        out_shape=jax.ShapeDtypeStruct((t, n), F32),
        scratch_shapes=[pltpu.VMEM((tm, k), BF16)],
        compiler_params=_params(("parallel", "arbitrary"), vmem),
    )(x, norm_w.reshape(1, k), w_stack)


def _matmul_residual_kernel(a_ref, w_ref, r_ref, o_ref):
    o_ref[...] = r_ref[...] + jnp.dot(a_ref[...], w_ref[...], preferred_element_type=F32)


def _matmul_residual(a, w_stack, layer, res):
    t, k = a.shape
    n = w_stack.shape[2]
    tm = _tile(t, TM_MATMUL, V7X_SUBLANES)
    tn = _tile(n, TN_MATMUL // 2, V7X_LANES)
    vmem = _vmem_limit([((tm, k), BF16), ((k, tn), BF16), ((tm, tn), F32), ((tm, tn), F32)])
    return pl.pallas_call(
        _matmul_residual_kernel,
        name="matmul_residual",
        grid=(t // tm, n // tn),
        in_specs=[pl.BlockSpec((tm, k), lambda i, j: (i, 0)),
                  pl.BlockSpec((None, k, tn), lambda i, j: (layer, 0, j)),
                  pl.BlockSpec((tm, tn), lambda i, j: (i, j))],
        out_specs=pl.BlockSpec((tm, tn), lambda i, j: (i, j)),
        out_shape=jax.ShapeDtypeStruct((t, n), F32),
        compiler_params=_params(("parallel", "arbitrary"), vmem),
    )(a, w_stack, res)


def _ffn_kernel(h_ref, nw_ref, wg_ref, wu_ref, wd_ref, o_ref, xn_ref, acc_ref):
    j = pl.program_id(1)

    @pl.when(j == 0)
    def _():
        xn_ref[...] = _rmsnorm_rows(h_ref[...], nw_ref[...]).astype(BF16)
        acc_ref[...] = jnp.zeros_like(acc_ref)

    xn = xn_ref[...]
    g = jnp.dot(xn, wg_ref[...], preferred_element_type=F32)
    u = jnp.dot(xn, wu_ref[...], preferred_element_type=F32)
    a = (_silu(g) * u).astype(BF16)
    acc_ref[...] += jnp.dot(a, wd_ref[...], preferred_element_type=F32)

    @pl.when(j == pl.num_programs(1) - 1)
    def _():
        o_ref[...] = h_ref[...] + 0.5 * acc_ref[...]


def _ffn(h, norm_w, w_gate_up, w_down, layer):
    t, d = h.shape
    f = w_down.shape[1]
    tm = _tile(t, TM_FFN, V7X_SUBLANES)
    tf = _tile(f, TF_FFN, V7X_LANES)
    nf = f // tf
    vmem = _vmem_limit([((tm, d), F32), ((d, tf), BF16), ((d, tf), BF16), ((tf, d), BF16), ((tm, d), F32)],
                       [((tm, d), BF16), ((tm, d), F32)])
    return pl.pallas_call(
        _ffn_kernel,
        name="swiglu_ffn",
        grid=(t // tm, nf),
        in_specs=[pl.BlockSpec((tm, d), lambda i, j: (i, 0)),
                  pl.BlockSpec((1, d), lambda i, j: (0, 0)),
                  pl.BlockSpec((None, d, tf), lambda i, j: (layer, 0, j)),
                  pl.BlockSpec((None, d, tf), lambda i, j: (layer, 0, j + nf)),
                  pl.BlockSpec((None, tf, d), lambda i, j: (layer, j, 0))],
        out_specs=pl.BlockSpec((tm, d), lambda i, j: (i, 0)),
        out_shape=jax.ShapeDtypeStruct((t, d), F32),
        scratch_shapes=[pltpu.VMEM((tm, d), BF16), pltpu.VMEM((tm, d), F32)],
        compiler_params=_params(("parallel", "arbitrary"), vmem),
    )(h, norm_w.reshape(1, d), w_gate_up, w_gate_up, w_down)


def _rmsnorm_kernel(x_ref, w_ref, o_ref):
    o_ref[...] = _rmsnorm_rows(x_ref[...], w_ref[...])


def _rmsnorm(x, w):
    t, d = x.shape
    tm = _tile(t, TM_ELEMWISE * 2, V7X_SUBLANES)
    return pl.pallas_call(
        _rmsnorm_kernel,
        name="final_rmsnorm",
        grid=(t // tm,),
        in_specs=[pl.BlockSpec((tm, d), lambda i: (i, 0)), pl.BlockSpec((1, d), lambda i: (0, 0))],
        out_specs=pl.BlockSpec((tm, d), lambda i: (i, 0)),
        out_shape=jax.ShapeDtypeStruct((t, d), F32),
        compiler_params=_params(("parallel",), _vmem_limit([((tm, d), F32), ((tm, d), F32)])),
    )(x, w.reshape(1, d))


def _halo_specs(ts, cw, n_tblk, where):
    per = ts // V7X_SUBLANES

    def prev_map(*idx):
        b, tb, c = where(*idx)
        return (b, jnp.maximum(tb * per - 1, 0), c)

    def next_map(*idx):
        b, tb, c = where(*idx)
        return (b, jnp.minimum((tb + 1) * per, n_tblk * per - 1), c)

    return [pl.BlockSpec((1, V7X_SUBLANES, cw), prev_map),
            pl.BlockSpec((1, ts, cw), where),
            pl.BlockSpec((1, V7X_SUBLANES, cw), next_map)]


def _conv_taps(prev_ref, x_ref, next_ref, w, xs_ref, ts, tblk, n_tblk):
    first = tblk == 0
    last = tblk == n_tblk - 1
    h = V7X_SUBLANES
    xs_ref[0:h] = jnp.where(first, 0.0, prev_ref[0])
    xs_ref[h:h + ts] = x_ref[0]
    xs_ref[h + ts:2 * h + ts] = jnp.where(last, 0.0, next_ref[0])
    y = None
    for j in range(CONV_WIDTH):
        off = h - CONV_LEFT + j
        term = xs_ref[off:off + ts] * w[j:j + 1]
        y = term if y is None else y + term
    return y


def _dn_conv_kernel(prev_ref, x_ref, next_ref, w_ref, o_ref, xs_ref, *, ts, n_q_blocks, n_qk_blocks):
    c = pl.program_id(2)
    y = _silu(_conv_taps(prev_ref, x_ref, next_ref, w_ref[...], xs_ref, ts, pl.program_id(1), pl.num_programs(1)))

    @pl.when(c < n_qk_blocks)
    def _():
        scale = jnp.where(c < n_q_blocks, HEAD_DIM ** -0.5, 1.0)
        for hh in range(y.shape[1] // HEAD_DIM):
            yh = y[:, hh * HEAD_DIM:(hh + 1) * HEAD_DIM]
            inv = lax.rsqrt(jnp.sum(yh * yh, axis=-1, keepdims=True) + RMS_EPS)
            o_ref[0, :, hh * HEAD_DIM:(hh + 1) * HEAD_DIM] = yh * inv * scale

    @pl.when(c >= n_qk_blocks)
    def _():
        o_ref[0] = y


def _dn_conv(proj, conv_w, key_dim):
    nb, s, _ = proj.shape
    c = conv_w.shape[1]
    ts = _tile(s, TS_CONV, V7X_SUBLANES)
    cw = _tile(key_dim, CW_CONV, HEAD_DIM)
    kern = functools.partial(_dn_conv_kernel, ts=ts, n_q_blocks=key_dim // cw, n_qk_blocks=2 * key_dim // cw)
    vmem = _vmem_limit([((ts, cw), F32), ((ts, cw), F32)], [((ts + 16, cw), F32), ((ts, cw), F32)])
    return pl.pallas_call(
        kern,
        name="dn_conv",
        grid=(nb, s // ts, c // cw),
        in_specs=(_halo_specs(ts, cw, s // ts, lambda b, i, cc: (b, i, cc))
                  + [pl.BlockSpec((CONV_WIDTH, cw), lambda b, i, cc: (0, cc))]),
        out_specs=pl.BlockSpec((1, ts, cw), lambda b, i, cc: (b, i, cc)),
        out_shape=jax.ShapeDtypeStruct((nb, s, c), F32),
        scratch_shapes=[pltpu.VMEM((ts + 2 * V7X_SUBLANES, cw), F32)],
        compiler_params=_params(("parallel", "parallel", "parallel"), vmem),
    )(proj, proj, proj, conv_w)


def _dn_gates_kernel(x_ref, nw_ref, w_ref, alog_ref, dtb_ref, o_ref, *, n_heads):
    xn = _rmsnorm_rows(x_ref[...], nw_ref[...]).astype(BF16)
    ba = jnp.dot(xn, w_ref[...], preferred_element_type=F32)
    lane = lax.broadcasted_iota(jnp.int32, ba.shape, 1)
    is_g = (lane // n_heads) % 2 == 1
    beta = jax.nn.sigmoid(ba)
    g = -jnp.exp(alog_ref[...]) * _softplus(ba + dtb_ref[...])
    o_ref[...] = jnp.where(is_g, g, beta)


def _dn_gates(x, norm_w, w_ba, a_log, dt_bias):
    t, k = x.shape
    n_heads = a_log.shape[-1]
    gl = 4 * n_heads
    zeros = jnp.zeros((2, 1, n_heads), F32)
    alog_l = jnp.concatenate([zeros, a_log.astype(F32)[:, None, :]], axis=1).reshape(1, gl)
    dtb_l = jnp.concatenate([zeros, dt_bias.astype(F32)[:, None, :]], axis=1).reshape(1, gl)
    tm = _tile(t, TM_MATMUL, V7X_SUBLANES)
    vmem = _vmem_limit([((tm, k), F32), ((k, gl), BF16), ((tm, gl), F32)], [((tm, k), BF16)])
    return pl.pallas_call(
        functools.partial(_dn_gates_kernel, n_heads=n_heads),
        name="dn_gates",
        grid=(t // tm,),
        in_specs=[pl.BlockSpec((tm, k), lambda i: (i, 0)),
                  pl.BlockSpec((1, k), lambda i: (0, 0)),
                  pl.BlockSpec((k, gl), lambda i: (0, 0)),
                  pl.BlockSpec((1, gl), lambda i: (0, 0)),
                  pl.BlockSpec((1, gl), lambda i: (0, 0))],
        out_specs=pl.BlockSpec((tm, gl), lambda i: (i, 0)),
        out_shape=jax.ShapeDtypeStruct((t, gl), F32),
        compiler_params=_params(("parallel",), vmem),
    )(x, norm_w.reshape(1, k), w_ba, alog_l, dtb_l)


def _chunk_cumsum(g, reverse):
    c = g.shape[0]
    row = lax.broadcasted_iota(jnp.int32, g.shape, 0)
    s = 1
    while s < c:
        if reverse:
            g = g + jnp.where(row < c - s, pltpu.roll(g, c - s, 0), 0.0)
        else:
            g = g + jnp.where(row >= s, pltpu.roll(g, s, 0), 0.0)
        s *= 2
    return g


def _square_pad(x, p):
    c, l = x.shape
    if l < p:
        x = jnp.concatenate([x, jnp.zeros((c, p - l), x.dtype)], axis=1)
    if c < p:
        x = jnp.concatenate([x] * (p // c), axis=0)
    return x


def _delta_rule_kernel(q_ref, k_ref, v_ref, gt_ref, *rest, reverse, n_chunks, group, heads, n_heads, gated_out):
    if gated_out:
        other_ref, z_ref, nw_ref, o_ref, *scratch = rest
    else:
        o_ref, *scratch = rest
    s_ref, u_ref, wq_ref, attn_ref, kdec_ref, egl_ref, gct_ref = scratch
    c = DN_CHUNK
    hg = pl.program_id(1)

    @pl.when(pl.program_id(2) == 0)
    def _():
        s_ref[...] = jnp.zeros_like(s_ref)

    row = lax.broadcasted_iota(jnp.int32, (c, 2 * c), 0)
    lane2 = lax.broadcasted_iota(jnp.int32, (c, 2 * c), 1)
    left = lane2 < c
    col = jnp.where(left, lane2, lane2 - c)
    eye_f = (row == col).astype(F32)
    incl = (col >= row) if reverse else (col <= row)
    strict = (col > row) if reverse else (col < row)
    same16 = (row // 16) == (col // 16)
    same32 = (row // 32) == (col // 32)
    m0 = (strict & same16).astype(F32)
    m1 = (strict & same32 & jnp.logical_not(same16)).astype(F32)
    m2 = (strict & jnp.logical_not(same32)).astype(F32)
    gl = gt_ref.shape[-1]
    lane = lax.broadcasted_iota(jnp.int32, (c, gl), 1)
    lane_beta0 = 2 * n_heads if reverse else 0
    lane_g0 = lane_beta0 + n_heads
    nt = (((1,), (1,)), ((), ()))
    tn = (((0,), (0,)), ((), ()))

    pad = gct_ref.shape[-1]
    rep = 2
    hd = HEAD_DIM

    def pick(tile, idx):
        return jnp.sum(jnp.where(lane == idx, tile, 0.0), axis=1, keepdims=True)

    def prep_body(gi, carry):
        rows_l, gates_l = [], []
        for ci in range(group):
            rows = pl.ds(pl.multiple_of((gi * group + ci) * c, c), c)
            gates = gt_ref[0, rows, :]
            gcum = _chunk_cumsum(gates, reverse)
            gct_ref[ci] = _square_pad(gcum, pad).T
            rows_l.append(rows)
            gates_l.append((gates, gcum))
        keys = [(ci, kh) for ci in range(group) for kh in range(heads // rep)]
        insts = [(ci, hv) for ci in range(group) for hv in range(heads)]
        k_d = {key: k_ref[0, rows_l[key[0]], key[1] * hd:(key[1] + 1) * hd] for key in keys}
        q_d = {key: q_ref[0, rows_l[key[0]], key[1] * hd:(key[1] + 1) * hd] for key in keys}
        beta_d, gc_d, kb_d, decay_d = {}, {}, {}, {}
        for key in keys:
            ci, kh = key
            gates, gcum = gates_l[ci]
            g_rows = []
            for r in range(rep):
                hv = kh * rep + r
                hv_global = hg * heads + hv
                beta_d[ci, hv] = pick(gates, lane_beta0 + hv_global)
                gc_d[ci, hv] = pick(gcum, lane_g0 + hv_global)
                kb_d[ci, hv] = k_d[key] * beta_d[ci, hv]
                g_rows.append(gct_ref[ci, pl.ds(lane_g0 + hv_global, 1), :][:, :2 * c])
            gc_p = jnp.where(left, gc_d[ci, kh * rep], gc_d[ci, kh * rep + 1])
            g_row_p = jnp.where(left[0:1], g_rows[0], g_rows[1])
            decay_d[key] = jnp.exp(jnp.where(incl, gc_p - g_row_p, -jnp.inf))
        kk_d, qk_d = {}, {}
        for key in keys:
            ci, kh = key
            kbf = k_d[key].astype(BF16)
            zero = jnp.zeros_like(kbf)
            k_diag = jnp.concatenate([jnp.concatenate([kbf, zero], axis=1),
                                      jnp.concatenate([zero, kbf], axis=1)], axis=0)
            lhs = jnp.concatenate([kb_d[ci, kh * rep], kb_d[ci, kh * rep + 1]], axis=1).astype(BF16)
            kk_d[key] = lax.dot_general(lhs, k_diag, nt, preferred_element_type=F32)
            k_twice = jnp.concatenate([kbf, kbf], axis=0)
            qk_d[key] = lax.dot_general(q_d[key].astype(BF16), k_twice, nt, preferred_element_type=F32)

        def block_diag(pair):
            pb = pair.astype(BF16)
            zero = jnp.zeros_like(pb)
            return jnp.concatenate([jnp.where(left, pb, zero), jnp.where(left, zero, pb)], axis=0)

        def pdot(pair, bd):
            return jnp.dot(pair.astype(BF16), bd, preferred_element_type=F32)

        lm_d, x_d, p_d = {}, {}, {}
        for key in keys:
            ci, kh = key
            lm_d[key] = jnp.where(strict, kk_d[key] * decay_d[key], 0.0)
            attn = (qk_d[key] * decay_d[key]).astype(BF16)
            for r in range(rep):
                attn_ref[gi * group + ci, kh * rep + r] = attn[:, r * c:(r + 1) * c]
            n0 = -(lm_d[key] * m0)
            x_d[key] = n0
            p_d[key] = eye_f + n0
        for key in keys:
            x_d[key] = pdot(x_d[key], block_diag(x_d[key]))
        for _ in range(2):
            for key in keys:
                both = pdot(jnp.concatenate([p_d[key], x_d[key]], axis=0), block_diag(x_d[key]))
                p_d[key] = p_d[key] + both[:c]
                x_d[key] = both[c:]
        for key in keys:
            p_d[key] = p_d[key] + pdot(p_d[key], block_diag(x_d[key]))
        for mask in (m1, m2):
            y_d = {key: pdot(p_d[key], block_diag(lm_d[key] * mask)) for key in keys}
            for key in keys:
                p_d[key] = p_d[key] - pdot(y_d[key], block_diag(p_d[key]))
        sol_d = {}
        for inst in insts:
            ci, hv = inst
            kh, r = hv // rep, hv % rep
            eg = jnp.exp(gc_d[inst])
            v = v_ref[0, rows_l[ci], hv * hd:(hv + 1) * hd]
            rhs = jnp.concatenate([v * beta_d[inst], kb_d[inst] * eg], axis=1)
            sol_d[inst] = _bdot(p_d[ci, kh][:, r * c:(r + 1) * c], rhs)
        for inst in insts:
            ci, hv = inst
            cc = gi * group + ci
            gc = gc_d[inst]
            g_last = gc[0:1] if reverse else gc[c - 1:c]
            eg = jnp.exp(gc)
            u_ref[cc, hv] = sol_d[inst][:, :hd]
            wq_ref[cc, hv] = jnp.concatenate([sol_d[inst][:, hd:], q_d[ci, hv // rep] * eg], axis=0).astype(BF16)
            kdec_ref[cc, hv] = (k_d[ci, hv // rep] * jnp.exp(g_last - gc)).astype(BF16)
            egl_ref[cc, hv] = jnp.broadcast_to(jnp.exp(g_last), egl_ref.shape[2:])
        return carry

    lax.fori_loop(0, n_chunks // group, prep_body, 0)

    def rec_body(ci, carry):
        cc = (n_chunks - 1 - ci) if reverse else ci
        rows = pl.ds(pl.multiple_of(cc * c, c), c)
        hs = range(heads)
        states = [s_ref[h] for h in hs]
        ws = [jnp.dot(wq_ref[cc, h], states[h].astype(BF16), preferred_element_type=F32) for h in hs]
        v_new = [(u_ref[cc, h] - ws[h][:c]).astype(BF16) for h in hs]
        av = [jnp.dot(attn_ref[cc, h], v_new[h], preferred_element_type=F32) for h in hs]
        kv = [lax.dot_general(kdec_ref[cc, h], v_new[h], tn, preferred_element_type=F32) for h in hs]
        for h in hs:
            cols = slice(h * hd, (h + 1) * hd)
            o = ws[h][c:] + av[h]
            if gated_out:
                o = _rmsnorm_rows(o + other_ref[0, rows, cols], nw_ref[...]) * _silu(z_ref[0, rows, cols])
            o_ref[0, rows, cols] = o.astype(o_ref.dtype)
            s_ref[h] = states[h] * egl_ref[cc, h][0:1, :] + kv[h]
        return carry

    lax.fori_loop(0, n_chunks, rec_body, 0)


def _delta_rule(qkv, gates, key_dim, n_heads, reverse, gated_out=None):
    nb, s, _ = qkv.shape
    gl = gates.shape[-1]
    heads = min(DN_HEADS_PER_STEP, n_heads)
    rep = n_heads // (key_dim // HEAD_DIM)
    assert rep == 2 and heads % rep == 0 and n_heads % heads == 0
    kw = (heads // rep) * HEAD_DIM
    vw = heads * HEAD_DIM
    n_chunks = _tile(s // DN_CHUNK, DN_CHUNKS_PER_STEP, 1)
    ts = n_chunks * DN_CHUNK
    nblk = s // ts
    tmap = (lambda cb: nblk - 1 - cb) if reverse else (lambda cb: cb)
    k_off = key_dim // kw
    v_off = 2 * key_dim // vw
    group = math.gcd(DN_PREP_GROUP, n_chunks)
    pad = max(2 * DN_CHUNK, gl)
    assert pad % DN_CHUNK == 0
    kern = functools.partial(_delta_rule_kernel, reverse=reverse, n_chunks=n_chunks, group=group, heads=heads,
                             n_heads=n_heads, gated_out=gated_out is not None)
    c = DN_CHUNK
    o_spec = pl.BlockSpec((1, ts, vw), lambda b, h, cb: (b, tmap(cb), h))
    in_specs = [pl.BlockSpec((1, ts, kw), lambda b, h, cb: (b, tmap(cb), h)),
                pl.BlockSpec((1, ts, kw), lambda b, h, cb: (b, tmap(cb), k_off + h)),
                pl.BlockSpec((1, ts, vw), lambda b, h, cb: (b, tmap(cb), v_off + h)),
                pl.BlockSpec((1, ts, gl), lambda b, h, cb: (b, tmap(cb), 0))]
    operands = [qkv, qkv, qkv, gates]
    blocks = [((ts, kw), F32), ((ts, kw), F32), ((ts, vw), F32), ((ts, gl), F32), ((ts, vw), F32)]
    out_dtype = F32
    if gated_out is not None:
        o_other, proj, z_off, out_norm_w = gated_out
        assert z_off % vw == 0
        in_specs += [o_spec,
                     pl.BlockSpec((1, ts, vw), lambda b, h, cb: (b, tmap(cb), z_off // vw + h)),
                     pl.BlockSpec((1, HEAD_DIM), lambda b, h, cb: (0, 0))]
        operands += [o_other, proj, out_norm_w.astype(F32).reshape(1, HEAD_DIM)]
        blocks += [((ts, vw), F32), ((ts, vw), F32)]
        out_dtype = BF16
    scratch = [((heads, HEAD_DIM, HEAD_DIM), F32),
               ((n_chunks, heads, c, HEAD_DIM), F32),
               ((n_chunks, heads, 2 * c, HEAD_DIM), BF16),
               ((n_chunks, heads, c, c), BF16),
               ((n_chunks, heads, c, HEAD_DIM), BF16),
               ((n_chunks, heads, V7X_SUBLANES, V7X_LANES), F32),
               ((group, pad, pad), F32)]
    vmem = _vmem_limit(blocks,
                       scratch + [((n_chunks, heads, c, V7X_LANES), BF16),
                                  ((group * heads, 16 * c, HEAD_DIM), F32)])
    return pl.pallas_call(
        kern,
        name="delta_rule_bwd" if reverse else "delta_rule_fwd",
        grid=(nb, n_heads // heads, nblk),
        in_specs=in_specs,
        out_specs=o_spec,
        out_shape=jax.ShapeDtypeStruct((nb, s, n_heads * HEAD_DIM), out_dtype),
        scratch_shapes=[pltpu.VMEM(s_, d_) for s_, d_ in scratch],
        compiler_params=_params(("parallel", "parallel", "arbitrary"), vmem),
    )(*operands)


def _deltanet_mixer(h, norm_w, w_in_stack, j, conv_w, a_log, dt_bias, out_norm_w, w_out_stack, nb):
    t, d = h.shape
    s = t // nb
    conv_dim = conv_w.shape[1]
    value_dim = w_out_stack.shape[1]
    key_dim = (conv_dim - value_dim) // 2
    n_heads = a_log.shape[-1]
    assert value_dim == n_heads * HEAD_DIM and conv_dim % value_dim == 0
    proj = _norm_matmul(h, norm_w, w_in_stack, j, conv_dim + value_dim)
    gates = _dn_gates(h, norm_w, w_in_stack[j, :, conv_dim + value_dim:], a_log, dt_bias)
    proj3 = proj.reshape(nb, s, -1)
    qkv = _dn_conv(proj3, conv_w.astype(F32), key_dim)
    gates = gates.reshape(nb, s, -1)
    o_f = _delta_rule(qkv, gates, key_dim, n_heads, reverse=False)
    y = _delta_rule(qkv, gates, key_dim, n_heads, reverse=True, gated_out=(o_f, proj3, conv_dim, out_norm_w))
    return _matmul_residual(y.reshape(t, value_dim), w_out_stack, j, h)


def _lru_scan_kernel(prev_ref, x_ref, next_ref, cw_ref, cb_ref, wg_ref, bg_ref, lam_ref, o_ref,
                     xs_ref, a_ref, b_ref, carry_ref, *, ts, reverse):
    step = pl.program_id(1)
    n_tblk = pl.num_programs(1)

    @pl.when(step == 0)
    def _():
        carry_ref[...] = jnp.zeros_like(carry_ref)

    nblocks, bw = wg_ref.shape[1], wg_ref.shape[2]
    tblk = (n_tblk - 1 - step) if reverse else step
    xs = _conv_taps(prev_ref, x_ref, next_ref, cw_ref[...], xs_ref, ts, tblk, n_tblk) + cb_ref[...]
    for blk in range(nblocks):
        cols = slice(blk * bw, (blk + 1) * bw)
        xc = xs[:, cols]
        gates = jnp.dot(xc.astype(BF16), wg_ref[0, blk], preferred_element_type=F32) + bg_ref[0, blk]
        r = jax.nn.sigmoid(gates[:, :bw])
        i = jax.nn.sigmoid(gates[:, bw:])
        log_a = -LRU_C * r * _softplus(-lam_ref[0, :, cols])
        a = jnp.exp(log_a)
        a_ref[:, cols] = a
        b_ref[:, cols] = jnp.sqrt(jnp.tanh(-log_a) * (a * a + 1.0)) * (i * xc)

    g = V7X_SUBLANES
    n_groups = ts // g
    width = a_ref.shape[1]
    cwid = math.gcd(width, LRU_SCAN_LANES)
    row = lax.broadcasted_iota(jnp.int32, (g, cwid), 0)
    for cchunk in range(width // cwid):
        cols = slice(cchunk * cwid, (cchunk + 1) * cwid)

        def group_body(gi, carry, cols=cols):
            gg = (n_groups - 1 - gi) if reverse else gi
            rows = pl.ds(pl.multiple_of(gg * g, g), g)
            av = a_ref[rows, cols]
            bv = b_ref[rows, cols]
            s = 1
            while s < g:
                if reverse:
                    valid = row < g - s
                    a_sh = jnp.where(valid, pltpu.roll(av, g - s, 0), 1.0)
                    b_sh = jnp.where(valid, pltpu.roll(bv, g - s, 0), 0.0)
                else:
                    valid = row >= s
                    a_sh = jnp.where(valid, pltpu.roll(av, s, 0), 1.0)
                    b_sh = jnp.where(valid, pltpu.roll(bv, s, 0), 0.0)
                bv = av * b_sh + bv
                av = av * a_sh
                s *= 2
            hv = av * carry + bv
            o_ref[0, rows, cols] = hv
            edge = hv[0:1] if reverse else hv[g - 1:g]
            return jnp.broadcast_to(edge, (g, cwid))

        carry_ref[:, cols] = lax.fori_loop(0, n_groups, group_body, carry_ref[:, cols],
                                           unroll=math.gcd(n_groups, LRU_SCAN_UNROLL))


def _lru_scan(proj, conv_w, conv_b, w_gates, b_gates, lam, reverse):
    nb, s, _ = proj.shape
    w = conv_w.shape[1]
    d = 1 if reverse else 0
    nblocks, bw = w_gates.shape[1], w_gates.shape[2]
    ts = _tile(s, TS_LRU, V7X_SUBLANES)
    nt = s // ts
    where = (lambda b, i: (b, nt - 1 - i, 0)) if reverse else (lambda b, i: (b, i, 0))
    in_specs = _halo_specs(ts, w, nt, where) + [
        pl.BlockSpec((CONV_WIDTH, w), lambda b, i: (0, 0)),
        pl.BlockSpec((1, w), lambda b, i: (0, 0)),
        pl.BlockSpec((1, nblocks, bw, 2 * bw), lambda b, i: (d, 0, 0, 0)),
        pl.BlockSpec((1, nblocks, 1, 2 * bw), lambda b, i: (d, 0, 0, 0)),
        pl.BlockSpec((1, 1, w), lambda b, i: (d, 0, 0)),
    ]
    kern = functools.partial(_lru_scan_kernel, ts=ts, reverse=reverse)
    vmem = _vmem_limit([((ts, w), F32), ((nblocks, bw, 2 * bw), BF16), ((ts, w), F32)],
                       [((ts + 16, w), F32), ((ts, w), F32), ((ts, w), F32), ((ts, w), F32)])
    return pl.pallas_call(
        kern,
        name="lru_scan_bwd" if reverse else "lru_scan_fwd",
        grid=(nb, nt),
        in_specs=in_specs,
        out_specs=pl.BlockSpec((1, ts, w), where),
        out_shape=jax.ShapeDtypeStruct((nb, s, w), F32),
        scratch_shapes=[pltpu.VMEM((ts + 2 * V7X_SUBLANES, w), F32), pltpu.VMEM((ts, w), F32),
                        pltpu.VMEM((ts, w), F32), pltpu.VMEM((V7X_SUBLANES, w), F32)],
        compiler_params=_params(("parallel", "arbitrary"), vmem),
    )(proj, proj, proj, conv_w, conv_b.reshape(1, w), w_gates, b_gates, lam.reshape(2, 1, w))


def _lru_out_gate_kernel(hf_ref, hb_ref, gate_ref, y_ref):
    y_ref[...] = ((hf_ref[...] + hb_ref[...]) * _gelu_tanh(gate_ref[...])).astype(BF16)


def _lru_out_gate(h_f, h_b, proj):
    t, w = h_f.shape
    tm = _tile(t, TM_ELEMWISE * 2, 2 * V7X_SUBLANES)
    vmem = _vmem_limit([((tm, w), F32)] * 3 + [((tm, w), BF16)])
    return pl.pallas_call(
        _lru_out_gate_kernel,
        name="lru_out_gate",
        grid=(t // tm,),
        in_specs=[pl.BlockSpec((tm, w), lambda i: (i, 0)),
                  pl.BlockSpec((tm, w), lambda i: (i, 0)),
                  pl.BlockSpec((tm, w), lambda i: (i, 1))],
        out_specs=pl.BlockSpec((tm, w), lambda i: (i, 0)),
        out_shape=jax.ShapeDtypeStruct((t, w), BF16),
        compiler_params=_params(("parallel",), vmem),
    )(h_f, h_b, proj)


def _rglru_mixer(h, norm_w, w_in_stack, j, conv_w, conv_b, w_gate_a, b_gate_a, w_gate_x, b_gate_x, lam,
                 w_out_stack, nb):
    t, d = h.shape
    s = t // nb
    w = conv_w.shape[1]
    nblocks, bw = w_gate_a.shape[1], w_gate_a.shape[2]
    proj = _norm_matmul(h, norm_w, w_in_stack, j, 2 * w)
    w_gates = jnp.concatenate([w_gate_a, w_gate_x], axis=-1).astype(BF16)
    b_gates = jnp.concatenate([b_gate_a.astype(F32).reshape(2, nblocks, bw),
                               b_gate_x.astype(F32).reshape(2, nblocks, bw)], axis=-1)[:, :, None, :]
    proj3 = proj.reshape(nb, s, 2 * w)
    args = (proj3, conv_w.astype(F32), conv_b.astype(F32), w_gates, b_gates, lam.astype(F32))
    h_f = _lru_scan(*args, reverse=False)
    h_b = _lru_scan(*args, reverse=True)
    y = _lru_out_gate(h_f.reshape(t, w), h_b.reshape(t, w), proj)
    return _matmul_residual(y, w_out_stack, j, h)


def kernel(x, ffn1_norm, ffn1_w_gate_up, ffn1_w_down, mix_norm, ffn2_norm, ffn2_w_gate_up, ffn2_w_down,
           dn_w_in, dn_conv_w, dn_a_log, dn_dt_bias, dn_out_norm, dn_w_out,
           lru_w_in, lru_conv_w, lru_conv_b, lru_w_gate_a, lru_b_gate_a, lru_w_gate_x, lru_b_gate_x,
           lru_lambda, lru_w_out, final_norm):
    nb, s, d = x.shape
    h = x.reshape(nb * s, d)
    ffn1_wgu, ffn1_wd = ffn1_w_gate_up.astype(BF16), ffn1_w_down.astype(BF16)
    ffn2_wgu, ffn2_wd = ffn2_w_gate_up.astype(BF16), ffn2_w_down.astype(BF16)
    dn_win, dn_wout = dn_w_in.astype(BF16), dn_w_out.astype(BF16)
    lru_win, lru_wout = lru_w_in.astype(BF16), lru_w_out.astype(BF16)
    for layer in range(ffn1_norm.shape[0]):
        h = _ffn(h, ffn1_norm[layer], ffn1_wgu, ffn1_wd, layer)
        j = layer // 2
        if layer % 2 == 0:
            h = _deltanet_mixer(h, mix_norm[layer], dn_win, j, dn_conv_w[j], dn_a_log[j], dn_dt_bias[j],
                                dn_out_norm[j], dn_wout, nb)
        else:
            h = _rglru_mixer(h, mix_norm[layer], lru_win, j, lru_conv_w[j], lru_conv_b[j], lru_w_gate_a[j],
                             lru_b_gate_a[j], lru_w_gate_x[j], lru_b_gate_x[j], lru_lambda[j], lru_wout, nb)
        h = _ffn(h, ffn2_norm[layer], ffn2_wgu, ffn2_wd, layer)
    return _rmsnorm(h, final_norm).reshape(nb, s, d)
```

```python
import functools
import math

import jax
import jax.numpy as jnp
from jax import lax
from jax.experimental import pallas as pl
from jax.experimental.pallas import tpu as pltpu

F32 = jnp.float32
BF16 = jnp.bfloat16

RMS_EPS = 1e-6
CONV_WIDTH = 4
CONV_LEFT = 2
DN_CHUNK = 64
HEAD_DIM = 128
LRU_C = 8.0

V7X_LANES = 128
V7X_SUBLANES = 8
V7X_VMEM_BYTES = 64 * 1024 * 1024
V7X_VMEM_USABLE = V7X_VMEM_BYTES - 6 * 1024 * 1024

TM_MATMUL = 1024
TN_MATMUL = 1024
TM_FFN = 512
TF_FFN = 512
TS_CONV = 512
CW_CONV = 512
CONV_ROW_CHUNK = 64
TS_LRU = 256
LRU_SCAN_LANES = 512
LRU_SCAN_UNROLL = 4
DN_HEADS_PER_STEP = 16
DN_CHUNKS_PER_STEP = 4
DN_PREP_GROUP = 4
TM_ELEMWISE = 256


def _tile(extent, preferred, quantum):
    if extent <= preferred:
        return extent
    t = (preferred // quantum) * quantum
    while t >= quantum:
        if extent % t == 0:
            return t
        t -= quantum
    return extent


def _vmem_limit(pipelined, scratch=()):
    nbytes = lambda s, d: math.prod(s) * jnp.dtype(d).itemsize
    total = 2 * sum(nbytes(s, d) for s, d in pipelined) + sum(nbytes(s, d) for s, d in scratch)
    return int(min(V7X_VMEM_USABLE, max(total * 5 // 4, 16 * 1024 * 1024)))


def _params(semantics, vmem):
    return pltpu.CompilerParams(dimension_semantics=semantics, vmem_limit_bytes=vmem)


def _bdot(a, b):
    return jnp.dot(a.astype(BF16), b.astype(BF16), preferred_element_type=F32)


def _rmsnorm_rows(x, w):
    return x * lax.rsqrt(jnp.mean(x * x, axis=-1, keepdims=True) + RMS_EPS) * w


def _softplus(x):
    return jnp.maximum(x, 0.0) + jnp.log1p(jnp.exp(-jnp.abs(x)))


def _silu(x):
    return x * jax.nn.sigmoid(x)


def _gelu_tanh(x):
    return 0.5 * x * (1.0 + jnp.tanh(math.sqrt(2.0 / math.pi) * (x + 0.044715 * (x * x * x))))


def _norm_matmul_kernel(x_ref, nw_ref, w_ref, o_ref, xn_ref):
    @pl.when(pl.program_id(1) == 0)
    def _():
        xn_ref[...] = _rmsnorm_rows(x_ref[...], nw_ref[...]).astype(BF16)

    o_ref[...] = jnp.dot(xn_ref[...], w_ref[...], preferred_element_type=F32)


def _norm_matmul(x, norm_w, w_stack, layer, n):
    t, k = x.shape
    tm = _tile(t, TM_MATMUL, V7X_SUBLANES)
    tn = _tile(n, TN_MATMUL, V7X_LANES)
    vmem = _vmem_limit([((tm, k), F32), ((k, tn), BF16), ((tm, tn), F32)], [((tm, k), BF16)])
    return pl.pallas_call(
        _norm_matmul_kernel,
        name="norm_matmul",
        grid=(t // tm, n // tn),
        in_specs=[pl.BlockSpec((tm, k), lambda i, j: (i, 0)),
                  pl.BlockSpec((1, k), lambda i, j: (0, 0)),
                  pl.BlockSpec((None, k, tn), lambda i, j: (layer, 0, j))],
        out_specs=pl.BlockSpec((tm, tn), lambda i, j: (i, j)),
        out_shape=jax.ShapeDtypeStruct((t, n), F32),
        scratch_shapes=[pltpu.VMEM((tm, k), BF16)],
        compiler_params=_params(("parallel", "arbitrary"), vmem),
    )(x, norm_w.reshape(1, k), w_stack)


def _matmul_residual_kernel(a_ref, w_ref, r_ref, o_ref):
    o_ref[...] = r_ref[...] + jnp.dot(a_ref[...], w_ref[...], preferred_element_type=F32)


def _matmul_residual(a, w_stack, layer, res):
    t, k = a.shape
    n = w_stack.shape[2]
    tm = _tile(t, TM_MATMUL, V7X_SUBLANES)
    tn = _tile(n, TN_MATMUL // 2, V7X_LANES)
    vmem = _vmem_limit([((tm, k), BF16), ((k, tn), BF16), ((tm, tn), F32), ((tm, tn), F32)])
    return pl.pallas_call(
        _matmul_residual_kernel,
        name="matmul_residual",
        grid=(t // tm, n // tn),
        in_specs=[pl.BlockSpec((tm, k), lambda i, j: (i, 0)),
                  pl.BlockSpec((None, k, tn), lambda i, j: (layer, 0, j)),
                  pl.BlockSpec((tm, tn), lambda i, j: (i, j))],
        out_specs=pl.BlockSpec((tm, tn), lambda i, j: (i, j)),
        out_shape=jax.ShapeDtypeStruct((t, n), F32),
        compiler_params=_params(("parallel", "arbitrary"), vmem),
    )(a, w_stack, res)


def _ffn_kernel(h_ref, nw_ref, wg_ref, wu_ref, wd_ref, *rest, out_norm):
    if out_norm:
        ow_ref, o_ref, xn_ref, acc_ref = rest
    else:
        o_ref, xn_ref, acc_ref = rest
    j = pl.program_id(1)

    @pl.when(j == 0)
    def _():
        xn_ref[...] = _rmsnorm_rows(h_ref[...], nw_ref[...]).astype(BF16)
        acc_ref[...] = jnp.zeros_like(acc_ref)

    xn = xn_ref[...]
    g = jnp.dot(xn, wg_ref[...], preferred_element_type=F32)
    u = jnp.dot(xn, wu_ref[...], preferred_element_type=F32)
    a = (_silu(g) * u).astype(BF16)
    acc_ref[...] += jnp.dot(a, wd_ref[...], preferred_element_type=F32)

    @pl.when(j == pl.num_programs(1) - 1)
    def _():
        res = h_ref[...] + 0.5 * acc_ref[...]
        o_ref[...] = _rmsnorm_rows(res, ow_ref[...]) if out_norm else res


def _ffn(h, norm_w, w_gate_up, w_down, layer, out_norm_w=None):
    t, d = h.shape
    f = w_down.shape[1]
    tm = _tile(t, TM_FFN, V7X_SUBLANES)
    tf = _tile(f, TF_FFN, V7X_LANES)
    nf = f // tf
    vmem = _vmem_limit([((tm, d), F32), ((d, tf), BF16), ((d, tf), BF16), ((tf, d), BF16), ((tm, d), F32)],
                       [((tm, d), BF16), ((tm, d), F32)])
    in_specs = [pl.BlockSpec((tm, d), lambda i, j: (i, 0)),
                pl.BlockSpec((1, d), lambda i, j: (0, 0)),
                pl.BlockSpec((None, d, tf), lambda i, j: (layer, 0, j)),
                pl.BlockSpec((None, d, tf), lambda i, j: (layer, 0, j + nf)),
                pl.BlockSpec((None, tf, d), lambda i, j: (layer, j, 0))]
    operands = [h, norm_w.reshape(1, d), w_gate_up, w_gate_up, w_down]
    if out_norm_w is not None:
        in_specs.append(pl.BlockSpec((1, d), lambda i, j: (0, 0)))
        operands.append(out_norm_w.astype(F32).reshape(1, d))
    return pl.pallas_call(
        functools.partial(_ffn_kernel, out_norm=out_norm_w is not None),
        name="swiglu_ffn",
        grid=(t // tm, nf),
        in_specs=in_specs,
        out_specs=pl.BlockSpec((tm, d), lambda i, j: (i, 0)),
        out_shape=jax.ShapeDtypeStruct((t, d), F32),
        scratch_shapes=[pltpu.VMEM((tm, d), BF16), pltpu.VMEM((tm, d), F32)],
        compiler_params=_params(("parallel", "arbitrary"), vmem),
    )(*operands)


def _halo_specs(ts, cw, n_tblk, where):
    per = ts // V7X_SUBLANES

    def prev_map(*idx):
        b, tb, c = where(*idx)
        return (b, jnp.maximum(tb * per - 1, 0), c)

    def next_map(*idx):
        b, tb, c = where(*idx)
        return (b, jnp.minimum((tb + 1) * per, n_tblk * per - 1), c)

    return [pl.BlockSpec((1, V7X_SUBLANES, cw), prev_map),
            pl.BlockSpec((1, ts, cw), where),
            pl.BlockSpec((1, V7X_SUBLANES, cw), next_map)]


def _conv_stage(prev_ref, x_ref, next_ref, xs_ref, ts, tblk, n_tblk):
    h = V7X_SUBLANES
    xs_ref[0:h] = jnp.where(tblk == 0, 0.0, prev_ref[0])
    xs_ref[h:h + ts] = x_ref[0]
    xs_ref[h + ts:2 * h + ts] = jnp.where(tblk == n_tblk - 1, 0.0, next_ref[0])


def _conv_chunks(xs_ref, w, ts):
    h = V7X_SUBLANES
    rc = math.gcd(ts, CONV_ROW_CHUNK)
    n = rc + 2 * h
    for r0 in range(0, ts, rc):
        blk = xs_ref[r0:r0 + n]
        y = None
        for j in range(CONV_WIDTH):
            sh = blk if j == CONV_LEFT else pltpu.roll(blk, (CONV_LEFT - j) % n, 0)
            term = sh[h:h + rc] * w[j:j + 1]
            y = term if y is None else y + term
        yield slice(r0, r0 + rc), y


def _dn_conv_kernel(prev_ref, x_ref, next_ref, w_ref, o_ref, xs_ref, *, ts, n_q_blocks, n_qk_blocks):
    c = pl.program_id(2)
    _conv_stage(prev_ref, x_ref, next_ref, xs_ref, ts, pl.program_id(1), pl.num_programs(1))
    w = w_ref[...]

    @pl.when(c < n_qk_blocks)
    def _():
        scale = jnp.where(c < n_q_blocks, HEAD_DIM ** -0.5, 1.0)
        for rows, y in _conv_chunks(xs_ref, w, ts):
            y = _silu(y)
            for hh in range(y.shape[1] // HEAD_DIM):
                yh = y[:, hh * HEAD_DIM:(hh + 1) * HEAD_DIM]
                inv = lax.rsqrt(jnp.sum(yh * yh, axis=-1, keepdims=True) + RMS_EPS)
                o_ref[0, rows, hh * HEAD_DIM:(hh + 1) * HEAD_DIM] = yh * inv * scale

    @pl.when(c >= n_qk_blocks)
    def _():
        for rows, y in _conv_chunks(xs_ref, w, ts):
            o_ref[0, rows, :] = _silu(y)


def _dn_conv(proj, conv_w, key_dim):
    nb, s, _ = proj.shape
    c = conv_w.shape[1]
    ts = _tile(s, TS_CONV, V7X_SUBLANES)
    cw = _tile(key_dim, CW_CONV, HEAD_DIM)
    kern = functools.partial(_dn_conv_kernel, ts=ts, n_q_blocks=key_dim // cw, n_qk_blocks=2 * key_dim // cw)
    vmem = _vmem_limit([((ts, cw), F32), ((ts, cw), F32)], [((ts + 16, cw), F32), ((ts, cw), F32)])
    return pl.pallas_call(
        kern,
        name="dn_conv",
        grid=(nb, s // ts, c // cw),
        in_specs=(_halo_specs(ts, cw, s // ts, lambda b, i, cc: (b, i, cc))
                  + [pl.BlockSpec((CONV_WIDTH, cw), lambda b, i, cc: (0, cc))]),
        out_specs=pl.BlockSpec((1, ts, cw), lambda b, i, cc: (b, i, cc)),
        out_shape=jax.ShapeDtypeStruct((nb, s, c), F32),
        scratch_shapes=[pltpu.VMEM((ts + 2 * V7X_SUBLANES, cw), F32)],
        compiler_params=_params(("parallel", "parallel", "parallel"), vmem),
    )(proj, proj, proj, conv_w)


def _dn_gates_kernel(x_ref, nw_ref, w_ref, alog_ref, dtb_ref, o_ref, *, n_heads):
    xn = _rmsnorm_rows(x_ref[...], nw_ref[...]).astype(BF16)
    ba = jnp.dot(xn, w_ref[...], preferred_element_type=F32)
    lane = lax.broadcasted_iota(jnp.int32, ba.shape, 1)
    is_g = (lane // n_heads) % 2 == 1
    beta = jax.nn.sigmoid(ba)
    g = -jnp.exp(alog_ref[...]) * _softplus(ba + dtb_ref[...])
    o_ref[...] = jnp.where(is_g, g, beta)


def _dn_gates(x, norm_w, w_ba, a_log, dt_bias):
    t, k = x.shape
    n_heads = a_log.shape[-1]
    gl = 4 * n_heads
    zeros = jnp.zeros((2, 1, n_heads), F32)
    alog_l = jnp.concatenate([zeros, a_log.astype(F32)[:, None, :]], axis=1).reshape(1, gl)
    dtb_l = jnp.concatenate([zeros, dt_bias.astype(F32)[:, None, :]], axis=1).reshape(1, gl)
    tm = _tile(t, TM_MATMUL, V7X_SUBLANES)
    vmem = _vmem_limit([((tm, k), F32), ((k, gl), BF16), ((tm, gl), F32)], [((tm, k), BF16)])
    return pl.pallas_call(
        functools.partial(_dn_gates_kernel, n_heads=n_heads),
        name="dn_gates",
        grid=(t // tm,),
        in_specs=[pl.BlockSpec((tm, k), lambda i: (i, 0)),
                  pl.BlockSpec((1, k), lambda i: (0, 0)),
                  pl.BlockSpec((k, gl), lambda i: (0, 0)),
                  pl.BlockSpec((1, gl), lambda i: (0, 0)),
                  pl.BlockSpec((1, gl), lambda i: (0, 0))],
        out_specs=pl.BlockSpec((tm, gl), lambda i: (i, 0)),
        out_shape=jax.ShapeDtypeStruct((t, gl), F32),
        compiler_params=_params(("parallel",), vmem),
    )(x, norm_w.reshape(1, k), w_ba, alog_l, dtb_l)


def _chunk_cumsum(g, reverse):
    c = g.shape[0]
    row = lax.broadcasted_iota(jnp.int32, g.shape, 0)
    s = 1
    while s < c:
        if reverse:
            g = g + jnp.where(row < c - s, pltpu.roll(g, c - s, 0), 0.0)
        else:
            g = g + jnp.where(row >= s, pltpu.roll(g, s, 0), 0.0)
        s *= 2
    return g


def _square_pad(x, p):
    c, l = x.shape
    if l < p:
        x = jnp.concatenate([x, jnp.zeros((c, p - l), x.dtype)], axis=1)
    if c < p:
        x = jnp.concatenate([x] * (p // c), axis=0)
    return x


def _delta_rule_kernel(q_ref, k_ref, v_ref, gt_ref, *rest, reverse, n_chunks, group, heads, n_heads, gated_out):
    if gated_out:
        other_ref, z_ref, nw_ref, o_ref, *scratch = rest
    else:
        o_ref, *scratch = rest
    s_ref, u_ref, wq_ref, attn_ref, kdec_ref, egl_ref, gct_ref = scratch
    c = DN_CHUNK
    hg = pl.program_id(1)

    @pl.when(pl.program_id(2) == 0)
    def _():
        s_ref[...] = jnp.zeros_like(s_ref)

    row = lax.broadcasted_iota(jnp.int32, (c, 2 * c), 0)
    lane2 = lax.broadcasted_iota(jnp.int32, (c, 2 * c), 1)
    left = lane2 < c
    col = jnp.where(left, lane2, lane2 - c)
    eye_f = (row == col).astype(F32)
    incl = (col >= row) if reverse else (col <= row)
    strict = (col > row) if reverse else (col < row)
    same16 = (row // 16) == (col // 16)
    same32 = (row // 32) == (col // 32)
    m0 = (strict & same16).astype(F32)
    m1 = (strict & same32 & jnp.logical_not(same16)).astype(F32)
    m2 = (strict & jnp.logical_not(same32)).astype(F32)
    gl = gt_ref.shape[-1]
    lane = lax.broadcasted_iota(jnp.int32, (c, gl), 1)
    lane_beta0 = 2 * n_heads if reverse else 0
    lane_g0 = lane_beta0 + n_heads
    nt = (((1,), (1,)), ((), ()))
    tn = (((0,), (0,)), ((), ()))

    pad = gct_ref.shape[-1]
    rep = 2
    hd = HEAD_DIM

    def pick(tile, idx):
        return jnp.sum(jnp.where(lane == idx, tile, 0.0), axis=1, keepdims=True)

    def prep_body(gi, carry):
        rows_l, gates_l = [], []
        for ci in range(group):
            rows = pl.ds(pl.multiple_of((gi * group + ci) * c, c), c)
            gates = gt_ref[0, rows, :]
            gcum = _chunk_cumsum(gates, reverse)
            gct_ref[ci] = _square_pad(gcum, pad).T
            rows_l.append(rows)
            gates_l.append((gates, gcum))
        keys = [(ci, kh) for ci in range(group) for kh in range(heads // rep)]
        insts = [(ci, hv) for ci in range(group) for hv in range(heads)]
        k_d = {key: k_ref[0, rows_l[key[0]], key[1] * hd:(key[1] + 1) * hd] for key in keys}
        q_d = {key: q_ref[0, rows_l[key[0]], key[1] * hd:(key[1] + 1) * hd] for key in keys}
        beta_d, gc_d, kb_d, decay_d = {}, {}, {}, {}
        for key in keys:
            ci, kh = key
            gates, gcum = gates_l[ci]
            g_rows = []
            for r in range(rep):
                hv = kh * rep + r
                hv_global = hg * heads + hv
                beta_d[ci, hv] = pick(gates, lane_beta0 + hv_global)
                gc_d[ci, hv] = pick(gcum, lane_g0 + hv_global)
                kb_d[ci, hv] = k_d[key] * beta_d[ci, hv]
                g_rows.append(gct_ref[ci, pl.ds(lane_g0 + hv_global, 1), :][:, :2 * c])
            gc_p = jnp.where(left, gc_d[ci, kh * rep], gc_d[ci, kh * rep + 1])
            g_row_p = jnp.where(left[0:1], g_rows[0], g_rows[1])
            decay_d[key] = jnp.exp(jnp.where(incl, gc_p - g_row_p, -jnp.inf))
        kk_d, qk_d = {}, {}
        for key in keys:
            ci, kh = key
            kbf = k_d[key].astype(BF16)
            zero = jnp.zeros_like(kbf)
            k_diag = jnp.concatenate([jnp.concatenate([kbf, zero], axis=1),
                                      jnp.concatenate([zero, kbf], axis=1)], axis=0)
            lhs = jnp.concatenate([kb_d[ci, kh * rep], kb_d[ci, kh * rep + 1]], axis=1).astype(BF16)
            kk_d[key] = lax.dot_general(lhs, k_diag, nt, preferred_element_type=F32)
            k_twice = jnp.concatenate([kbf, kbf], axis=0)
            qk_d[key] = lax.dot_general(q_d[key].astype(BF16), k_twice, nt, preferred_element_type=F32)

        def block_diag(pair):
            pb = pair.astype(BF16)
            zero = jnp.zeros_like(pb)
            return jnp.concatenate([jnp.where(left, pb, zero), jnp.where(left, zero, pb)], axis=0)

        def pdot(pair, bd):
            return jnp.dot(pair.astype(BF16), bd, preferred_element_type=F32)

        lm_d, x_d, p_d = {}, {}, {}
        for key in keys:
            ci, kh = key
            lm_d[key] = jnp.where(strict, kk_d[key] * decay_d[key], 0.0)
            attn = (qk_d[key] * decay_d[key]).astype(BF16)
            for r in range(rep):
                attn_ref[gi * group + ci, kh * rep + r] = attn[:, r * c:(r + 1) * c]
            n0 = -(lm_d[key] * m0)
            x_d[key] = n0
            p_d[key] = eye_f + n0
        for key in keys:
            x_d[key] = pdot(x_d[key], block_diag(x_d[key]))
        for _ in range(2):
            for key in keys:
                both = pdot(jnp.concatenate([p_d[key], x_d[key]], axis=0), block_diag(x_d[key]))
                p_d[key] = p_d[key] + both[:c]
                x_d[key] = both[c:]
        for key in keys:
            p_d[key] = p_d[key] + pdot(p_d[key], block_diag(x_d[key]))
        for mask in (m1, m2):
            y_d = {key: pdot(p_d[key], block_diag(lm_d[key] * mask)) for key in keys}
            for key in keys:
                p_d[key] = p_d[key] - pdot(y_d[key], block_diag(p_d[key]))
        sol_d = {}
        for inst in insts:
            ci, hv = inst
            kh, r = hv // rep, hv % rep
            eg = jnp.exp(gc_d[inst])
            v = v_ref[0, rows_l[ci], hv * hd:(hv + 1) * hd]
            rhs = jnp.concatenate([v * beta_d[inst], kb_d[inst] * eg], axis=1)
            sol_d[inst] = _bdot(p_d[ci, kh][:, r * c:(r + 1) * c], rhs)
        for inst in insts:
            ci, hv = inst
            cc = gi * group + ci
            gc = gc_d[inst]
            g_last = gc[0:1] if reverse else gc[c - 1:c]
            eg = jnp.exp(gc)
            u_ref[cc, hv] = sol_d[inst][:, :hd]
            wq_ref[cc, hv] = jnp.concatenate([sol_d[inst][:, hd:], q_d[ci, hv // rep] * eg], axis=0).astype(BF16)
            kdec_ref[cc, hv] = (k_d[ci, hv // rep] * jnp.exp(g_last - gc)).astype(BF16)
            egl_ref[cc, hv] = jnp.broadcast_to(jnp.exp(g_last), egl_ref.shape[2:])
        return carry

    lax.fori_loop(0, n_chunks // group, prep_body, 0)

    def rec_body(ci, carry):
        cc = (n_chunks - 1 - ci) if reverse else ci
        rows = pl.ds(pl.multiple_of(cc * c, c), c)
        hs = range(heads)
        states = [s_ref[h] for h in hs]
        ws = [jnp.dot(wq_ref[cc, h], states[h].astype(BF16), preferred_element_type=F32) for h in hs]
        v_new = [(u_ref[cc, h] - ws[h][:c]).astype(BF16) for h in hs]
        av = [jnp.dot(attn_ref[cc, h], v_new[h], preferred_element_type=F32) for h in hs]
        kv = [lax.dot_general(kdec_ref[cc, h], v_new[h], tn, preferred_element_type=F32) for h in hs]
        for h in hs:
            cols = slice(h * hd, (h + 1) * hd)
            o = ws[h][c:] + av[h]
            if gated_out:
                o = _rmsnorm_rows(o + other_ref[0, rows, cols], nw_ref[...]) * _silu(z_ref[0, rows, cols])
            o_ref[0, rows, cols] = o.astype(o_ref.dtype)
            s_ref[h] = states[h] * egl_ref[cc, h][0:1, :] + kv[h]
        return carry

    lax.fori_loop(0, n_chunks, rec_body, 0)


def _delta_rule(qkv, gates, key_dim, n_heads, reverse, gated_out=None):
    nb, s, _ = qkv.shape
    gl = gates.shape[-1]
    heads = min(DN_HEADS_PER_STEP, n_heads)
    rep = n_heads // (key_dim // HEAD_DIM)
    assert rep == 2 and heads % rep == 0 and n_heads % heads == 0
    kw = (heads // rep) * HEAD_DIM
    vw = heads * HEAD_DIM
    n_chunks = _tile(s // DN_CHUNK, DN_CHUNKS_PER_STEP, 1)
    ts = n_chunks * DN_CHUNK
    nblk = s // ts
    tmap = (lambda cb: nblk - 1 - cb) if reverse else (lambda cb: cb)
    k_off = key_dim // kw
    v_off = 2 * key_dim // vw
    group = math.gcd(DN_PREP_GROUP, n_chunks)
    pad = max(2 * DN_CHUNK, gl)
    assert pad % DN_CHUNK == 0
    kern = functools.partial(_delta_rule_kernel, reverse=reverse, n_chunks=n_chunks, group=group, heads=heads,
                             n_heads=n_heads, gated_out=gated_out is not None)
    c = DN_CHUNK
    o_spec = pl.BlockSpec((1, ts, vw), lambda b, h, cb: (b, tmap(cb), h))
    in_specs = [pl.BlockSpec((1, ts, kw), lambda b, h, cb: (b, tmap(cb), h)),
                pl.BlockSpec((1, ts, kw), lambda b, h, cb: (b, tmap(cb), k_off + h)),
                pl.BlockSpec((1, ts, vw), lambda b, h, cb: (b, tmap(cb), v_off + h)),
                pl.BlockSpec((1, ts, gl), lambda b, h, cb: (b, tmap(cb), 0))]
    operands = [qkv, qkv, qkv, gates]
    blocks = [((ts, kw), F32), ((ts, kw), F32), ((ts, vw), F32), ((ts, gl), F32), ((ts, vw), F32)]
    out_dtype = F32
    if gated_out is not None:
        o_other, proj, z_off, out_norm_w = gated_out
        assert z_off % vw == 0
        in_specs += [o_spec,
                     pl.BlockSpec((1, ts, vw), lambda b, h, cb: (b, tmap(cb), z_off // vw + h)),
                     pl.BlockSpec((1, HEAD_DIM), lambda b, h, cb: (0, 0))]
        operands += [o_other, proj, out_norm_w.astype(F32).reshape(1, HEAD_DIM)]
        blocks += [((ts, vw), F32), ((ts, vw), F32)]
        out_dtype = BF16
    scratch = [((heads, HEAD_DIM, HEAD_DIM), F32),
               ((n_chunks, heads, c, HEAD_DIM), F32),
               ((n_chunks, heads, 2 * c, HEAD_DIM), BF16),
               ((n_chunks, heads, c, c), BF16),
               ((n_chunks, heads, c, HEAD_DIM), BF16),
               ((n_chunks, heads, V7X_SUBLANES, V7X_LANES), F32),
               ((group, pad, pad), F32)]
    vmem = _vmem_limit(blocks,
                       scratch + [((n_chunks, heads, c, V7X_LANES), BF16),
                                  ((group * heads, 16 * c, HEAD_DIM), F32)])
    return pl.pallas_call(
        kern,
        name="delta_rule_bwd" if reverse else "delta_rule_fwd",
        grid=(nb, n_heads // heads, nblk),
        in_specs=in_specs,
        out_specs=o_spec,
        out_shape=jax.ShapeDtypeStruct((nb, s, n_heads * HEAD_DIM), out_dtype),
        scratch_shapes=[pltpu.VMEM(s_, d_) for s_, d_ in scratch],
        compiler_params=_params(("parallel", "parallel", "arbitrary"), vmem),
    )(*operands)


def _deltanet_mixer(h, norm_w, w_in_stack, j, conv_w, a_log, dt_bias, out_norm_w, w_out_stack, nb):
    t, d = h.shape
    s = t // nb
    conv_dim = conv_w.shape[1]
    value_dim = w_out_stack.shape[1]
    key_dim = (conv_dim - value_dim) // 2
    n_heads = a_log.shape[-1]
    assert value_dim == n_heads * HEAD_DIM and conv_dim % value_dim == 0
    proj = _norm_matmul(h, norm_w, w_in_stack, j, conv_dim + value_dim)
    gates = _dn_gates(h, norm_w, w_in_stack[j, :, conv_dim + value_dim:], a_log, dt_bias)
    proj3 = proj.reshape(nb, s, -1)
    qkv = _dn_conv(proj3, conv_w.astype(F32), key_dim)
    gates = gates.reshape(nb, s, -1)
    o_f = _delta_rule(qkv, gates, key_dim, n_heads, reverse=False)
    y = _delta_rule(qkv, gates, key_dim, n_heads, reverse=True, gated_out=(o_f, proj3, conv_dim, out_norm_w))
    return _matmul_residual(y.reshape(t, value_dim), w_out_stack, j, h)


def _lru_scan_kernel(prev_ref, x_ref, next_ref, cw_ref, cb_ref, wg_ref, bg_ref, lam_ref, o_ref,
                     xs_ref, a_ref, b_ref, carry_ref, *, ts, reverse):
    step = pl.program_id(1)
    n_tblk = pl.num_programs(1)

    @pl.when(step == 0)
    def _():
        carry_ref[...] = jnp.zeros_like(carry_ref)

    nblocks, bw = wg_ref.shape[1], wg_ref.shape[2]
    tblk = (n_tblk - 1 - step) if reverse else step
    _conv_stage(prev_ref, x_ref, next_ref, xs_ref, ts, tblk, n_tblk)
    for rows, y in _conv_chunks(xs_ref, cw_ref[...], ts):
        b_ref[rows, :] = y + cb_ref[...]
    for blk in range(nblocks):
        cols = slice(blk * bw, (blk + 1) * bw)
        xc = b_ref[:, cols]
        gates = jnp.dot(xc.astype(BF16), wg_ref[0, blk], preferred_element_type=F32) + bg_ref[0, blk]
        r = jax.nn.sigmoid(gates[:, :bw])
        i = jax.nn.sigmoid(gates[:, bw:])
        log_a = -LRU_C * r * _softplus(-lam_ref[0, :, cols])
        a = jnp.exp(log_a)
        a_ref[:, cols] = a
        b_ref[:, cols] = jnp.sqrt(jnp.tanh(-log_a) * (a * a + 1.0)) * (i * xc)

    g = V7X_SUBLANES
    n_groups = ts // g
    width = a_ref.shape[1]
    cwid = math.gcd(width, LRU_SCAN_LANES)
    row = lax.broadcasted_iota(jnp.int32, (g, cwid), 0)
    for cchunk in range(width // cwid):
        cols = slice(cchunk * cwid, (cchunk + 1) * cwid)

        def group_body(gi, carry, cols=cols):
            gg = (n_groups - 1 - gi) if reverse else gi
            rows = pl.ds(pl.multiple_of(gg * g, g), g)
            av = a_ref[rows, cols]
            bv = b_ref[rows, cols]
            s = 1
            while s < g:
                if reverse:
                    valid = row < g - s
                    a_sh = jnp.where(valid, pltpu.roll(av, g - s, 0), 1.0)
                    b_sh = jnp.where(valid, pltpu.roll(bv, g - s, 0), 0.0)
                else:
                    valid = row >= s
                    a_sh = jnp.where(valid, pltpu.roll(av, s, 0), 1.0)
                    b_sh = jnp.where(valid, pltpu.roll(bv, s, 0), 0.0)
                bv = av * b_sh + bv
                av = av * a_sh
                s *= 2
            hv = av * carry + bv
            o_ref[0, rows, cols] = hv
            edge = hv[0:1] if reverse else hv[g - 1:g]
            return jnp.broadcast_to(edge, (g, cwid))

        carry_ref[:, cols] = lax.fori_loop(0, n_groups, group_body, carry_ref[:, cols],
                                           unroll=math.gcd(n_groups, LRU_SCAN_UNROLL))


def _lru_scan(proj, conv_w, conv_b, w_gates, b_gates, lam, reverse):
    nb, s, _ = proj.shape
    w = conv_w.shape[1]
    d = 1 if reverse else 0
    nblocks, bw = w_gates.shape[1], w_gates.shape[2]
    ts = _tile(s, TS_LRU, V7X_SUBLANES)
    nt = s // ts
    where = (lambda b, i: (b, nt - 1 - i, 0)) if reverse else (lambda b, i: (b, i, 0))
    in_specs = _halo_specs(ts, w, nt, where) + [
        pl.BlockSpec((CONV_WIDTH, w), lambda b, i: (0, 0)),
        pl.BlockSpec((1, w), lambda b, i: (0, 0)),
        pl.BlockSpec((1, nblocks, bw, 2 * bw), lambda b, i: (d, 0, 0, 0)),
        pl.BlockSpec((1, nblocks, 1, 2 * bw), lambda b, i: (d, 0, 0, 0)),
        pl.BlockSpec((1, 1, w), lambda b, i: (d, 0, 0)),
    ]
    kern = functools.partial(_lru_scan_kernel, ts=ts, reverse=reverse)
    vmem = _vmem_limit([((ts, w), F32), ((nblocks, bw, 2 * bw), BF16), ((ts, w), F32)],
                       [((ts + 16, w), F32), ((ts, w), F32), ((ts, w), F32), ((ts, w), F32)])
    return pl.pallas_call(
        kern,
        name="lru_scan_bwd" if reverse else "lru_scan_fwd",
        grid=(nb, nt),
        in_specs=in_specs,
        out_specs=pl.BlockSpec((1, ts, w), where),
        out_shape=jax.ShapeDtypeStruct((nb, s, w), F32),
        scratch_shapes=[pltpu.VMEM((ts + 2 * V7X_SUBLANES, w), F32), pltpu.VMEM((ts, w), F32),
                        pltpu.VMEM((ts, w), F32), pltpu.VMEM((V7X_SUBLANES, w), F32)],
        compiler_params=_params(("parallel", "arbitrary"), vmem),
    )(proj, proj, proj, conv_w, conv_b.reshape(1, w), w_gates, b_gates, lam.reshape(2, 1, w))


def _lru_out_gate_kernel(hf_ref, hb_ref, gate_ref, y_ref):
    y_ref[...] = ((hf_ref[...] + hb_ref[...]) * _gelu_tanh(gate_ref[...])).astype(BF16)


def _lru_out_gate(h_f, h_b, proj):
    t, w = h_f.shape
    tm = _tile(t, TM_ELEMWISE * 2, 2 * V7X_SUBLANES)
    vmem = _vmem_limit([((tm, w), F32)] * 3 + [((tm, w), BF16)])
    return pl.pallas_call(
        _lru_out_gate_kernel,
        name="lru_out_gate",
        grid=(t // tm,),
        in_specs=[pl.BlockSpec((tm, w), lambda i: (i, 0)),
                  pl.BlockSpec((tm, w), lambda i: (i, 0)),
                  pl.BlockSpec((tm, w), lambda i: (i, 1))],
        out_specs=pl.BlockSpec((tm, w), lambda i: (i, 0)),
        out_shape=jax.ShapeDtypeStruct((t, w), BF16),
        compiler_params=_params(("parallel",), vmem),
    )(h_f, h_b, proj)


def _rglru_mixer(h, norm_w, w_in_stack, j, conv_w, conv_b, w_gate_a, b_gate_a, w_gate_x, b_gate_x, lam,
                 w_out_stack, nb):
    t, d = h.shape
    s = t // nb
    w = conv_w.shape[1]
    nblocks, bw = w_gate_a.shape[1], w_gate_a.shape[2]
    proj = _norm_matmul(h, norm_w, w_in_stack, j, 2 * w)
    w_gates = jnp.concatenate([w_gate_a, w_gate_x], axis=-1).astype(BF16)
    b_gates = jnp.concatenate([b_gate_a.astype(F32).reshape(2, nblocks, bw),
                               b_gate_x.astype(F32).reshape(2, nblocks, bw)], axis=-1)[:, :, None, :]
    proj3 = proj.reshape(nb, s, 2 * w)
    args = (proj3, conv_w.astype(F32), conv_b.astype(F32), w_gates, b_gates, lam.astype(F32))
    h_f = _lru_scan(*args, reverse=False)
    h_b = _lru_scan(*args, reverse=True)
    y = _lru_out_gate(h_f.reshape(t, w), h_b.reshape(t, w), proj)
    return _matmul_residual(y, w_out_stack, j, h)


def kernel(x, ffn1_norm, ffn1_w_gate_up, ffn1_w_down, mix_norm, ffn2_norm, ffn2_w_gate_up, ffn2_w_down,
           dn_w_in, dn_conv_w, dn_a_log, dn_dt_bias, dn_out_norm, dn_w_out,
           lru_w_in, lru_conv_w, lru_conv_b, lru_w_gate_a, lru_b_gate_a, lru_w_gate_x, lru_b_gate_x,
           lru_lambda, lru_w_out, final_norm):
    nb, s, d = x.shape
    h = x.reshape(nb * s, d)
    ffn1_wgu, ffn1_wd = ffn1_w_gate_up.astype(BF16), ffn1_w_down.astype(BF16)
    ffn2_wgu, ffn2_wd = ffn2_w_gate_up.astype(BF16), ffn2_w_down.astype(BF16)
    dn_win, dn_wout = dn_w_in.astype(BF16), dn_w_out.astype(BF16)
    lru_win, lru_wout = lru_w_in.astype(BF16), lru_w_out.astype(BF16)
    for layer in range(ffn1_norm.shape[0]):
        h = _ffn(h, ffn1_norm[layer], ffn1_wgu, ffn1_wd, layer)
        j = layer // 2
        if layer % 2 == 0:
            h = _deltanet_mixer(h, mix_norm[layer], dn_win, j, dn_conv_w[j], dn_a_log[j], dn_dt_bias[j],
                                dn_out_norm[j], dn_wout, nb)
        else:
            h = _rglru_mixer(h, mix_norm[layer], lru_win, j, lru_conv_w[j], lru_conv_b[j], lru_w_gate_a[j],
                             lru_b_gate_a[j], lru_w_gate_x[j], lru_b_gate_x[j], lru_lambda[j], lru_wout, nb)
        last = layer == ffn1_norm.shape[0] - 1
        h = _ffn(h, ffn2_norm[layer], ffn2_wgu, ffn2_wd, layer, out_norm_w=final_norm if last else None)
    return h.reshape(nb, s, d)
```
